```python
import math
import jax, jax.numpy as jnp
from jax import lax
import numpy as np

D_MODEL = 1024
BATCH = 32
SEQ = 2048
DEPTH = 4

N_A_LAYERS = DEPTH // 2
N_B_LAYERS = DEPTH - N_A_LAYERS
D_FF = 4 * D_MODEL

A_HEADS = 4
A_QK_DIM = D_MODEL // (2 * A_HEADS)
A_V_DIM = D_MODEL // A_HEADS
A_CHUNK = 64
A_Q_COLS = A_HEADS * A_QK_DIM
A_V_COLS = A_HEADS * A_V_DIM
A_IN_COLS = 2 * A_Q_COLS + 2 * A_V_COLS + 2 * A_HEADS

B_HEADS = 8
B_HEAD_DIM = D_MODEL // (2 * B_HEADS)
B_V_DIM = 2 * B_HEAD_DIM
B_Q_COLS = B_HEADS * 2 * B_HEAD_DIM
B_K_COLS = B_HEADS * 2 * B_HEAD_DIM
B_KV_COLS = B_K_COLS + B_HEADS * B_V_DIM
Q_BLOCK = 128
ROPE_THETA = 500000.0
ROPE_DIM = B_HEAD_DIM // 4

NORM_EPS = 1e-6
SUBLN_EPS = 1e-5

kernel_name = "yoco_mlstm_diffattn_hybrid"


def lambda_init(layer_number):
    return 0.8 - 0.6 * math.exp(-0.3 * (layer_number - 1))


def rms_norm(x, g, eps=NORM_EPS):
    xf = x.astype(jnp.float32)
    y = xf * lax.rsqrt(jnp.mean(jnp.square(xf), axis=-1, keepdims=True) + eps)
    return (y * g.astype(jnp.float32)).astype(x.dtype)


def modulate(y, shift, scale):
    return y * (1 + scale) + shift


def sq_relu_mlp(y, w_up, w_down):
    return jnp.square(jax.nn.relu(y @ w_up)) @ w_down


def rope_tables(positions):
    inv_freq = jnp.power(ROPE_THETA, -jnp.arange(0, ROPE_DIM, 2, dtype=jnp.float32) / ROPE_DIM)
    ang = positions.astype(jnp.float32)[..., None] * inv_freq
    return jnp.cos(ang), jnp.sin(ang)


def apply_partial_rope(t, cos, sin):
    shape = cos.shape[:2] + (1,) * (t.ndim - 3) + (cos.shape[-1],)
    cos = cos.reshape(shape).astype(t.dtype)
    sin = sin.reshape(shape).astype(t.dtype)
    half = ROPE_DIM // 2
    r1 = t[..., :half]
    r2 = t[..., half:ROPE_DIM]
    return jnp.concatenate([r1 * cos - r2 * sin, r2 * cos + r1 * sin, t[..., ROPE_DIM:]], axis=-1)


def mlstm_chunkwise(q, k, v, i_pre, log_f):
    B, S, H, DK = q.shape
    DV = v.shape[-1]
    L = A_CHUNK
    NC = S // L
    f32 = jnp.float32

    def chunk_vec(t):
        return t.astype(f32).reshape(B, NC, L, H, t.shape[-1]).transpose(1, 0, 3, 2, 4)

    def chunk_gate(t):
        return t.astype(f32).reshape(B, NC, L, H).transpose(1, 0, 3, 2)

    qc, kc, vc = chunk_vec(q), chunk_vec(k), chunk_vec(v)
    ic = chunk_gate(i_pre)
    gc = jnp.cumsum(chunk_gate(log_f), axis=-1)
    causal = jnp.tril(jnp.ones((L, L), dtype=bool))

    def step(carry, xs):
        C, n, m = carry
        qb, kb, vb, ib, gb = xs
        G = gb[..., -1]
        Dm = jnp.where(causal, gb[..., :, None] - gb[..., None, :] + ib[..., None, :], -jnp.inf)
        inter = gb + m[..., None]
        m_row = jnp.maximum(inter, jnp.max(Dm, axis=-1))
        w_intra = jnp.exp(Dm - m_row[..., None])
        w_inter = jnp.exp(inter - m_row)
        s = jnp.einsum('bhld,bhsd->bhls', qb, kb) * w_intra
        num = jnp.einsum('bhls,bhsv->bhlv', s, vb) + w_inter[..., None] * jnp.einsum('bhld,bhdv->bhlv', qb, C)
        den = jnp.sum(s, axis=-1) + w_inter * jnp.einsum('bhld,bhd->bhl', qb, n)
        h = num / jnp.maximum(jnp.abs(den), jnp.exp(-m_row))[..., None]
        a = G[..., None] - gb + ib
        m_new = jnp.maximum(G + m, jnp.max(a, axis=-1))
        w_old = jnp.exp(G + m - m_new)
        w_s = jnp.exp(a - m_new[..., None])
        C_new = w_old[..., None, None] * C + jnp.einsum('bhs,bhsd,bhsv->bhdv', w_s, kb, vb)
        n_new = w_old[..., None] * n + jnp.einsum('bhs,bhsd->bhd', w_s, kb)
        return (C_new, n_new, m_new), h

    init = (jnp.zeros((B, H, DK, DV), f32), jnp.zeros((B, H, DK), f32), jnp.zeros((B, H), f32))
    _, hs = lax.scan(step, init, (qc, kc, vc, ic, gc))
    return hs.transpose(1, 0, 3, 2, 4).reshape(B, S, H, DV).astype(q.dtype)


def mlstm_mixer(y, w_in, b_gates, head_g, w_out):
    B, S, _ = y.shape
    proj = y @ w_in
    o1 = A_Q_COLS
    o2 = o1 + A_Q_COLS
    o3 = o2 + A_V_COLS
    o4 = o3 + A_V_COLS
    o5 = o4 + A_HEADS
    q = proj[..., :o1].reshape(B, S, A_HEADS, A_QK_DIM)
    k = proj[..., o1:o2].reshape(B, S, A_HEADS, A_QK_DIM) * (A_QK_DIM ** -0.5)
    v = proj[..., o2:o3].reshape(B, S, A_HEADS, A_V_DIM)
    o_pre = proj[..., o3:o4]
    i_pre = (proj[..., o4:o5] + b_gates[0]).astype(jnp.float32)
    log_f = jax.nn.log_sigmoid((proj[..., o5:] + b_gates[1]).astype(jnp.float32))
    h = mlstm_chunkwise(q, k, v, i_pre, log_f)
    h = rms_norm(h, head_g.reshape(A_HEADS, A_V_DIM))
    h = h.reshape(B, S, A_V_COLS) * jax.nn.sigmoid(o_pre)
    return h @ w_out


def shared_kv(h, c_act, kv_norm_g, kv_ada_w, kv_ada_b, kv_w, cos, sin):
    B, S, _ = h.shape
    sh, sc = jnp.split((c_act @ kv_ada_w + kv_ada_b)[:, None, :], 2, axis=-1)
    y = modulate(rms_norm(h, kv_norm_g), sh, sc)
    kv = y @ kv_w
    k = kv[..., :B_K_COLS].reshape(B, S, B_HEADS, 2, B_HEAD_DIM)
    k = apply_partial_rope(k, cos, sin).transpose(0, 2, 3, 1, 4)
    v = kv[..., B_K_COLS:].reshape(B, S, B_HEADS, B_V_DIM).transpose(0, 2, 1, 3)
    return k[:, :, 0], k[:, :, 1], v


def diff_attention(y, k1, k2, v, w_q, lam_params, subln_g, w_out, cos, sin, lam_init):
    B, S, _ = y.shape
    q = (y @ w_q).reshape(B, S, B_HEADS, 2, B_HEAD_DIM)
    q = apply_partial_rope(q, cos, sin) * (B_HEAD_DIM ** -0.5)
    q = q.transpose(0, 2, 3, 1, 4)
    q1, q2 = q[:, :, 0], q[:, :, 1]
    lp = lam_params.astype(jnp.float32)
    lam = jnp.exp(jnp.sum(lp[0] * lp[1])) - jnp.exp(jnp.sum(lp[2] * lp[3])) + lam_init
    outs = []
    for blk in range(S // Q_BLOCK):
        q0 = blk * Q_BLOCK
        kv_len = q0 + Q_BLOCK
        mask = jnp.arange(kv_len)[None, :] <= jnp.arange(q0, kv_len)[:, None]
        s1 = jnp.einsum('bhqd,bhkd->bhqk', q1[:, :, q0:kv_len], k1[:, :, :kv_len]).astype(jnp.float32)
        s2 = jnp.einsum('bhqd,bhkd->bhqk', q2[:, :, q0:kv_len], k2[:, :, :kv_len]).astype(jnp.float32)
        p1 = jax.nn.softmax(jnp.where(mask, s1, -jnp.inf), axis=-1)
        p2 = jax.nn.softmax(jnp.where(mask, s2, -jnp.inf), axis=-1)
        a = (p1 - lam * p2).astype(v.dtype)
        outs.append(jnp.einsum('bhqk,bhkv->bhqv', a, v[:, :, :kv_len]))
    o = jnp.concatenate(outs, axis=2)
    o = rms_norm(o, subln_g, SUBLN_EPS) * (1 - lam_init)
    o = o.transpose(0, 2, 1, 3).reshape(B, S, B_HEADS * B_V_DIM)
    return o @ w_out


def setup_inputs(seed: int = 0) -> dict:
    key = jax.random.key(seed)
    ks = jax.random.split(key, 24)
    nrm = jax.random.normal
    f32 = jnp.float32
    x = nrm(ks[0], (BATCH, SEQ, D_MODEL), f32)
    c = nrm(ks[1], (BATCH, D_MODEL), f32)
    positions = (jnp.arange(SEQ, dtype=jnp.int32)[None, :]
                 + jax.random.randint(ks[2], (BATCH, 1), 0, 1024, dtype=jnp.int32))
    ada_w = nrm(ks[3], (DEPTH, D_MODEL, 6 * D_MODEL), f32) * (0.5 * D_MODEL ** -0.5)
    ada_b = 0.02 * nrm(ks[4], (DEPTH, 6 * D_MODEL), f32)
    norm_mix_g = 1.0 + 0.02 * nrm(ks[5], (DEPTH, D_MODEL), f32)
    norm_mlp_g = 1.0 + 0.02 * nrm(ks[6], (DEPTH, D_MODEL), f32)
    mlp_w_up = nrm(ks[7], (DEPTH, D_MODEL, D_FF), f32) * D_MODEL ** -0.5
    mlp_w_down = nrm(ks[8], (DEPTH, D_FF, D_MODEL), f32) * D_FF ** -0.5
    mlstm_w_in = nrm(ks[9], (N_A_LAYERS, D_MODEL, A_IN_COLS), f32) * D_MODEL ** -0.5
    b_i = 0.1 * nrm(ks[10], (N_A_LAYERS, A_HEADS), f32)
    b_f = jnp.linspace(3.0, 6.0, A_HEADS, dtype=f32)[None, :] + 0.1 * nrm(ks[11], (N_A_LAYERS, A_HEADS), f32)
    mlstm_b_gates = jnp.stack([b_i, b_f], axis=1)
    mlstm_head_g = 1.0 + 0.02 * nrm(ks[12], (N_A_LAYERS, A_V_COLS), f32)
    mlstm_w_out = nrm(ks[13], (N_A_LAYERS, A_V_COLS, D_MODEL), f32) * A_V_COLS ** -0.5
    kv_norm_g = 1.0 + 0.02 * nrm(ks[14], (D_MODEL,), f32)
    kv_ada_w = nrm(ks[15], (D_MODEL, 2 * D_MODEL), f32) * (0.5 * D_MODEL ** -0.5)
    kv_ada_b = 0.02 * nrm(ks[16], (2 * D_MODEL,), f32)
    kv_w = nrm(ks[17], (D_MODEL, B_KV_COLS), f32) * D_MODEL ** -0.5
    diff_w_q = nrm(ks[18], (N_B_LAYERS, D_MODEL, B_Q_COLS), f32) * D_MODEL ** -0.5
    diff_lambda = 0.1 * nrm(ks[19], (N_B_LAYERS, 4, B_HEAD_DIM), f32)
    diff_subln_g = 1.0 + 0.02 * nrm(ks[20], (N_B_LAYERS, B_V_DIM), f32)
    diff_w_out = nrm(ks[21], (N_B_LAYERS, B_HEADS * B_V_DIM, D_MODEL), f32) * (B_HEADS * B_V_DIM) ** -0.5
    final_norm_g = 1.0 + 0.02 * nrm(ks[22], (D_MODEL,), f32)
    return {"x": x, "c": c, "positions": positions,
            "ada_w": ada_w, "ada_b": ada_b,
            "norm_mix_g": norm_mix_g, "norm_mlp_g": norm_mlp_g,
            "mlp_w_up": mlp_w_up, "mlp_w_down": mlp_w_down,
            "mlstm_w_in": mlstm_w_in, "mlstm_b_gates": mlstm_b_gates,
            "mlstm_head_g": mlstm_head_g, "mlstm_w_out": mlstm_w_out,
            "kv_norm_g": kv_norm_g, "kv_ada_w": kv_ada_w, "kv_ada_b": kv_ada_b, "kv_w": kv_w,
            "diff_w_q": diff_w_q, "diff_lambda": diff_lambda,
            "diff_subln_g": diff_subln_g, "diff_w_out": diff_w_out,
            "final_norm_g": final_norm_g}


def reference(x, c, positions, ada_w, ada_b, norm_mix_g, norm_mlp_g, mlp_w_up, mlp_w_down,
              mlstm_w_in, mlstm_b_gates, mlstm_head_g, mlstm_w_out,
              kv_norm_g, kv_ada_w, kv_ada_b, kv_w,
              diff_w_q, diff_lambda, diff_subln_g, diff_w_out, final_norm_g):
    c_act = jax.nn.silu(c)
    cos, sin = rope_tables(positions)
    k1 = k2 = v = None
    for layer in range(DEPTH):
        mod = (c_act @ ada_w[layer] + ada_b[layer])[:, None, :]
        sh1, sc1, g1, sh2, sc2, g2 = jnp.split(mod, 6, axis=-1)
        y = modulate(rms_norm(x, norm_mix_g[layer]), sh1, sc1)
        if layer < N_A_LAYERS:
            mix = mlstm_mixer(y, mlstm_w_in[layer], mlstm_b_gates[layer],
                              mlstm_head_g[layer], mlstm_w_out[layer])
        else:
            j = layer - N_A_LAYERS
            mix = diff_attention(y, k1, k2, v, diff_w_q[j], diff_lambda[j], diff_subln_g[j],
                                 diff_w_out[j], cos, sin, lambda_init(layer + 1))
        x = x + g1 * mix
        y = modulate(rms_norm(x, norm_mlp_g[layer]), sh2, sc2)
        x = x + g2 * sq_relu_mlp(y, mlp_w_up[layer], mlp_w_down[layer])
        if layer == N_A_LAYERS - 1:
            k1, k2, v = shared_kv(x, c_act, kv_norm_g, kv_ada_w, kv_ada_b, kv_w, cos, sin)
    return rms_norm(x, final_norm_g)
```

```python
import functools
import math

import jax
import jax.numpy as jnp
from jax import lax
from jax.experimental import pallas as pl
from jax.experimental.pallas import tpu as pltpu

F32 = jnp.float32
BF16 = jnp.bfloat16

NORM_EPS = 1e-6
SUBLN_EPS = 1e-5
ROPE_THETA = 500000.0
MLSTM_CHUNK = 64

V7X_VMEM_LIMIT_BYTES = 52 * 1024 * 1024
LANES = 128

_NT = (((1,), (1,)), ((), ()))
_TN = (((0,), (0,)), ((), ()))


def _tile(n, pref):
    t = min(n, pref)
    assert n % t == 0, (n, t)
    return t


def _params(sem):
    return pltpu.CompilerParams(dimension_semantics=sem, vmem_limit_bytes=V7X_VMEM_LIMIT_BYTES)


def _resident(shape):
    nd = len(shape)
    return pl.BlockSpec(shape, lambda *_: (0,) * nd, pipeline_mode=pl.Buffered(1))


def _rms(x, eps):
    return x * lax.rsqrt(jnp.mean(x * x, axis=-1, keepdims=True) + eps)


def _norm_mod(x, g, shift, scale):
    return (_rms(x, NORM_EPS) * g) * (1.0 + scale) + shift


def _mods_kernel(c_ref, w_ref, b_ref, o_ref):
    c = c_ref[...]
    act = (c * jax.nn.sigmoid(c)).astype(BF16)
    o_ref[0] = jnp.dot(act, w_ref[0].astype(BF16), preferred_element_type=F32) + b_ref[0]


def _mods(c, w, b):
    nl, d, n = w.shape
    bsz = c.shape[0]
    tn = _tile(n, 1024)
    return pl.pallas_call(
        _mods_kernel,
        grid=(nl, n // tn),
        in_specs=[
            pl.BlockSpec((bsz, d), lambda l, j: (0, 0)),
            pl.BlockSpec((1, d, tn), lambda l, j: (l, 0, j)),
            pl.BlockSpec((1, 1, tn), lambda l, j: (l, 0, j)),
        ],
        out_specs=pl.BlockSpec((1, bsz, tn), lambda l, j: (l, 0, j)),
        out_shape=jax.ShapeDtypeStruct((nl, bsz, n), F32),
        compiler_params=_params(("arbitrary", "arbitrary")),
        name="mods",
    )(c, w, b.reshape(nl, 1, n))


def _rope_table_kernel(pos_ref, cos_ref, sin_ref, *, comp_dim, rope_dim):
    half = rope_dim // 2
    pos = pos_ref[...].astype(F32)
    lane = lax.broadcasted_iota(jnp.int32, (1, LANES), 1) % comp_dim
    fidx = (lane % half).astype(F32)
    inv_freq = jnp.exp(fidx * (-2.0 * math.log(ROPE_THETA) / rope_dim))
    ang = pos * inv_freq
    rot = lane < rope_dim
    cos_ref[...] = jnp.where(rot, jnp.cos(ang), 1.0)
    sin_ref[...] = jnp.where(rot, jnp.where(lane < half, -jnp.sin(ang), jnp.sin(ang)), 0.0)


def _rope_tables(positions, comp_dim, rope_dim):
    t = positions.size
    tm = _tile(t, 2048)
    return pl.pallas_call(
        functools.partial(_rope_table_kernel, comp_dim=comp_dim, rope_dim=rope_dim),
        grid=(t // tm,),
        in_specs=[pl.BlockSpec((tm, 1), lambda i: (i, 0))],
        out_specs=[pl.BlockSpec((tm, LANES), lambda i: (i, 0))] * 2,
        out_shape=[jax.ShapeDtypeStruct((t, LANES), F32)] * 2,
        compiler_params=_params(("arbitrary",)),
        name="rope_tables",
    )(positions.reshape(t, 1))


def _apply_rope(r, cos_t, sin_t, half):
    lane = lax.broadcasted_iota(jnp.int32, cos_t.shape, 1)
    first = (lane % (LANES // 2)) < half
    outs = []
    for j in range(r.shape[1] // LANES):
        blk = r[:, j * LANES:(j + 1) * LANES]
        partner = jnp.where(first, pltpu.roll(blk, LANES - half, axis=1), pltpu.roll(blk, half, axis=1))
        outs.append(blk * cos_t + partner * sin_t)
    return outs


def _rope_proj_kernel(x_ref, g_ref, sh_ref, sc_ref, w_ref, cos_ref, sin_ref, *o_refs, half, rope_outs, out_scale, nc):
    y = _norm_mod(x_ref[0], g_ref[...], sh_ref[0], sc_ref[0]).astype(BF16)
    cos_t = cos_ref[...]
    sin_t = sin_ref[...]
    for oi, o_ref in enumerate(o_refs):
        width = o_ref.shape[-1]
        for c in range(width // nc):
            col = oi * width + c * nc
            r = jnp.dot(y, w_ref[:, col:col + nc], preferred_element_type=F32)
            if oi < rope_outs:
                blocks = _apply_rope(r, cos_t, sin_t, half)
            else:
                blocks = [r[:, j * LANES:(j + 1) * LANES] for j in range(nc // LANES)]
            for j, blk in enumerate(blocks):
                if out_scale != 1.0:
                    blk = blk * out_scale
                o_ref[0, :, c * nc + j * LANES:c * nc + (j + 1) * LANES] = blk.astype(o_ref.dtype)


def _rope_proj(x, g, shift, scale, w, cos_t, sin_t, *, n_out, rope_outs, out_scale, half):
    bsz, s, d = x.shape
    n = w.shape[1]
    width = n // n_out
    tm = _tile(s, 512)
    nt = s // tm
    kern = functools.partial(_rope_proj_kernel, half=half, rope_outs=rope_outs, out_scale=out_scale,
                             nc=_tile(width, 512))
    vec = pl.BlockSpec((1, 1, d), lambda b, i: (b, 0, 0))
    tab = pl.BlockSpec((tm, LANES), lambda b, i: (b * nt + i, 0))
    return pl.pallas_call(
        kern,
        grid=(bsz, nt),
        in_specs=[
            pl.BlockSpec((1, tm, d), lambda b, i: (b, i, 0)),
            _resident((1, d)), vec, vec,
            _resident((d, n)),
            tab, tab,
        ],
        out_specs=[pl.BlockSpec((1, tm, width), lambda b, i: (b, i, 0))] * n_out,
        out_shape=[jax.ShapeDtypeStruct((bsz, s, width), BF16)] * n_out,
        compiler_params=_params(("arbitrary", "arbitrary")),
        name="rope_proj",
    )(x, g.reshape(1, d), shift, scale, w, cos_t, sin_t)


def _mlstm_proj_kernel(x_ref, g_ref, sh_ref, sc_ref, w_ref, wg_ref, bg_ref, qkvo_ref, gates_ref, *, q_cols, v_cols, k_scale, heads, nc, chunk):
    y = _norm_mod(x_ref[0], g_ref[...], sh_ref[0], sc_ref[0]).astype(BF16)
    n = w_ref.shape[1]
    for c in range(n // nc):
        col = c * nc
        r = jnp.dot(y, w_ref[:, col:col + nc], preferred_element_type=F32)
        if q_cols <= col < 2 * q_cols:
            r = r * k_scale
        elif col >= 2 * q_cols + v_cols:
            r = jax.nn.sigmoid(r)
        qkvo_ref[0, :, col:col + nc] = r.astype(BF16)
    gt = lax.dot_general(wg_ref[...], y, _NT, preferred_element_type=F32) + bg_ref[...]
    row = lax.broadcasted_iota(jnp.int32, gt.shape, 0)
    logf = jnp.minimum(gt, 0.0) - jnp.log1p(jnp.exp(-jnp.abs(gt)))
    gt = jnp.where(row < heads, gt, logf)
    for c in range(gt.shape[1] // chunk):
        gates_ref[0, c] = gt[:, c * chunk:(c + 1) * chunk]


def _mlstm_proj(x, g, shift, scale, w_main, w_gates_t, b_gates, *, q_cols, v_cols, k_scale, heads, chunk):
    bsz, s, d = x.shape
    n = w_main.shape[1]
    tm = _tile(s, 512)
    kern = functools.partial(_mlstm_proj_kernel, q_cols=q_cols, v_cols=v_cols, k_scale=k_scale, heads=heads,
                             nc=_tile(q_cols, 512), chunk=chunk)
    vec = pl.BlockSpec((1, 1, d), lambda b, i: (b, 0, 0))
    return pl.pallas_call(
        kern,
        grid=(bsz, s // tm),
        in_specs=[
            pl.BlockSpec((1, tm, d), lambda b, i: (b, i, 0)),
            _resident((1, d)), vec, vec,
            _resident((d, n)),
            _resident((2 * heads, d)),
            _resident((2 * heads, 1)),
        ],
        out_specs=[
            pl.BlockSpec((1, tm, n), lambda b, i: (b, i, 0)),
            pl.BlockSpec((1, tm // chunk, 2 * heads, chunk), lambda b, i: (b, i, 0, 0)),
        ],
        out_shape=[
            jax.ShapeDtypeStruct((bsz, s, n), BF16),
            jax.ShapeDtypeStruct((bsz, s // chunk, 2 * heads, chunk), F32),
        ],
        compiler_params=_params(("arbitrary", "arbitrary")),
        name="mlstm_proj",
    )(x, g.reshape(1, d), shift, scale, w_main, w_gates_t, b_gates)


def _mlstm_kernel(q_ref, k_ref, v_ref, og_ref, gates_ref, hg_ref, out_ref, c_sc, n_sc, m_sc, *, heads, dk, dv, chunk):
    @pl.when(pl.program_id(1) == 0)
    def _():
        c_sc[...] = jnp.zeros_like(c_sc)
        n_sc[...] = jnp.zeros_like(n_sc)
        m_sc[...] = jnp.zeros_like(m_sc)

    L = chunk
    row = lax.broadcasted_iota(jnp.int32, (L, L), 0)
    col = lax.broadcasted_iota(jnp.int32, (L, L), 1)
    causal = col <= row
    upper = (row <= col).astype(F32)
    sel = jnp.concatenate([causal.astype(F32), (row == col).astype(F32)], axis=0)
    hi = lax.Precision.HIGHEST

    def body(c, carry):
        r0 = pl.multiple_of(c * L, L)
        gr = gates_ref[0, c]
        cum = jnp.dot(gr, upper, precision=hi, preferred_element_type=F32)
        cols = lax.dot_general(sel, gr, _NT, precision=hi, preferred_element_type=F32)
        for h in range(heads):
            i_row = gr[h:h + 1, :]
            g_row = cum[heads + h:heads + h + 1, :]
            g_col = cols[:L, heads + h:heads + h + 1]
            i_col = cols[L:, h:h + 1]
            g_tot = g_row[:, L - 1:L]
            m_old = m_sc[h]
            qh = q_ref[0, pl.ds(r0, L), h * dk:(h + 1) * dk]
            kh = k_ref[0, pl.ds(r0, L), h * dk:(h + 1) * dk]
            vh = v_ref[0, pl.ds(r0, L), h * dv:(h + 1) * dv]
            oh = og_ref[0, pl.ds(r0, L), h * dv:(h + 1) * dv]
            c_old = c_sc[h]
            n_old = n_sc[h]

            dm = jnp.where(causal, g_col - g_row + i_row, -jnp.inf)
            inter = g_col + m_old
            m_row = jnp.maximum(inter, jnp.max(dm, axis=-1, keepdims=True))
            w_intra = jnp.exp(dm - m_row)
            w_inter = jnp.exp(inter - m_row)
            s = lax.dot_general(qh, kh, _NT, preferred_element_type=F32) * w_intra
            num = jnp.dot(s.astype(BF16), vh, preferred_element_type=F32)
            num = num + w_inter * jnp.dot(qh, c_old.astype(BF16), preferred_element_type=F32)
            qn = jnp.sum(qh.astype(F32) * n_old.astype(BF16).astype(F32), axis=-1, keepdims=True)
            den = jnp.sum(s, axis=-1, keepdims=True) + w_inter * qn
            hh = num / jnp.maximum(jnp.abs(den), jnp.exp(-m_row))
            hh = _rms(hh, NORM_EPS) * hg_ref[:, h * dv:(h + 1) * dv]
            out_ref[0, pl.ds(r0, L), h * dv:(h + 1) * dv] = (hh * oh.astype(F32)).astype(out_ref.dtype)

            a_row = g_tot - g_row + i_row
            a_col = g_tot - g_col + i_col
            m_new = jnp.maximum(g_tot + m_old, jnp.max(a_row, axis=-1, keepdims=True))
            w_old = jnp.exp(g_tot + m_old - m_new)
            kw = kh.astype(F32) * jnp.exp(a_col - m_new)
            c_sc[h] = w_old * c_old + lax.dot_general(kw.astype(BF16), vh, _TN, preferred_element_type=F32)
            n_sc[h] = w_old * n_old + jnp.sum(kw, axis=0, keepdims=True)
            m_sc[h] = m_new
        return carry

    lax.fori_loop(0, q_ref.shape[1] // L, body, 0)


def _mlstm(qkvo, gates, head_g, *, heads, dk, dv, chunk):
    bsz, s, _ = qkvo.shape
    ts = _tile(s, 512)
    qc, vc = heads * dk, heads * dv
    kern = functools.partial(_mlstm_kernel, heads=heads, dk=dk, dv=dv, chunk=chunk)
    return pl.pallas_call(
        kern,
        grid=(bsz, s // ts),
        in_specs=[
            pl.BlockSpec((1, ts, qc), lambda b, i: (b, i, 0)),
            pl.BlockSpec((1, ts, qc), lambda b, i: (b, i, 1)),
            pl.BlockSpec((1, ts, vc), lambda b, i: (b, i, (2 * qc) // vc)),
            pl.BlockSpec((1, ts, vc), lambda b, i: (b, i, (2 * qc) // vc + 1)),
            pl.BlockSpec((1, ts // chunk, 2 * heads, chunk), lambda b, i: (b, i, 0, 0)),
            _resident((1, vc)),
        ],
        out_specs=pl.BlockSpec((1, ts, vc), lambda b, i: (b, i, 0)),
        out_shape=jax.ShapeDtypeStruct((bsz, s, vc), BF16),
        scratch_shapes=[
            pltpu.VMEM((heads, dk, dv), F32),
            pltpu.VMEM((heads, 1, dk), F32),
            pltpu.VMEM((heads, 1, 1), F32),
        ],
        compiler_params=_params(("arbitrary", "arbitrary")),
        name="mlstm",
    )(qkvo, qkvo, qkvo, qkvo, gates, head_g.reshape(1, vc))


def _attn_kernel(q_ref, k_ref, v_ref, lam_ref, g_ref, o_ref, m_sc, l_sc, acc_sc, *, tq, tk, comp_dim, lam_init):
    qi = pl.program_id(2)
    q = q_ref[0]
    lane = lax.broadcasted_iota(jnp.int32, q.shape, 1)
    zero = jnp.zeros_like(q)
    qq = jnp.concatenate([jnp.where(lane < comp_dim, q, zero), jnp.where(lane >= comp_dim, q, zero)], axis=0)

    m_sc[...] = jnp.full_like(m_sc, -jnp.inf)
    l_sc[...] = jnp.zeros_like(l_sc)
    acc_sc[...] = jnp.zeros_like(acc_sc)

    def step(j, masked):
        k0 = pl.multiple_of(j * tk, tk)
        kk = k_ref[0, pl.ds(k0, tk), :]
        vv = v_ref[0, pl.ds(k0, tk), :]
        s = lax.dot_general(qq, kk, _NT, preferred_element_type=F32)
        if masked:
            r = lax.broadcasted_iota(jnp.int32, s.shape, 0) % tq + qi * tq
            c = lax.broadcasted_iota(jnp.int32, s.shape, 1) + j * tk
            s = jnp.where(c <= r, s, -jnp.inf)
        m_old = m_sc[...]
        m_new = jnp.maximum(m_old, jnp.max(s, axis=-1, keepdims=True))
        alpha = jnp.exp(m_old - m_new)
        p = jnp.exp(s - m_new)
        l_sc[...] = alpha * l_sc[...] + jnp.sum(p, axis=-1, keepdims=True)
        acc_sc[...] = alpha * acc_sc[...] + jnp.dot(p.astype(BF16), vv, preferred_element_type=F32)
        m_sc[...] = m_new

    per_q = tq // tk
    n_full = qi * per_q

    def full_body(j, carry):
        step(j, False)
        return carry

    lax.fori_loop(0, n_full, full_body, 0)
    for d in range(per_q):
        step(n_full + d, True)

    lp = lam_ref[...]
    lam = (jnp.exp(jnp.sum(lp[0:1] * lp[1:2], axis=-1, keepdims=True))
           - jnp.exp(jnp.sum(lp[2:3] * lp[3:4], axis=-1, keepdims=True)) + lam_init)
    o = acc_sc[...] / l_sc[...]
    o = o[:tq] - lam * o[tq:]
    o = _rms(o, SUBLN_EPS) * g_ref[...] * (1.0 - lam_init)
    o_ref[0] = o.astype(o_ref.dtype)


def _attention(q, k, v, lam_params, subln_g, *, heads, lam_init):
    bsz, s, d = q.shape
    hd = d // heads
    tq = _tile(s, 256)
    tk = tq
    kern = functools.partial(_attn_kernel, tq=tq, tk=tk, comp_dim=hd // 2, lam_init=lam_init)
    kv = pl.BlockSpec((1, s, hd), lambda b, h, i: (b, 0, h))
    return pl.pallas_call(
        kern,
        grid=(bsz, heads, s // tq),
        in_specs=[
            pl.BlockSpec((1, tq, hd), lambda b, h, i: (b, i, h)),
            kv, kv,
            _resident(lam_params.shape),
            _resident((1, hd)),
        ],
        out_specs=pl.BlockSpec((1, tq, hd), lambda b, h, i: (b, i, h)),
        out_shape=jax.ShapeDtypeStruct((bsz, s, d), BF16),
        scratch_shapes=[
            pltpu.VMEM((2 * tq, 1), F32),
            pltpu.VMEM((2 * tq, 1), F32),
            pltpu.VMEM((2 * tq, hd), F32),
        ],
        compiler_params=_params(("arbitrary", "arbitrary", "arbitrary")),
        name="diff_attn",
    )(q, k, v, lam_params, subln_g.reshape(1, hd))


def _out_mlp_kernel(x_ref, a_ref, g1_ref, sh_ref, sc_ref, g2_ref, ng_ref, wo_ref, wu_ref, wd_ref, fg_ref, o_ref, *, fc, final):
    x1 = x_ref[0] + g1_ref[0] * jnp.dot(a_ref[0], wo_ref[...], preferred_element_type=F32)
    y = _norm_mod(x1, ng_ref[...], sh_ref[0], sc_ref[0]).astype(BF16)
    acc = jnp.zeros_like(x1)
    for c in range(wu_ref.shape[1] // fc):
        h = jnp.dot(y, wu_ref[:, c * fc:(c + 1) * fc], preferred_element_type=F32)
        h = jnp.square(jnp.maximum(h, 0.0)).astype(BF16)
        acc = acc + jnp.dot(h, wd_ref[c * fc:(c + 1) * fc, :], preferred_element_type=F32)
    x2 = x1 + g2_ref[0] * acc
    if final:
        x2 = _rms(x2, NORM_EPS) * fg_ref[...]
    o_ref[0] = x2


def _out_mlp(x, a, gate1, shift2, scale2, gate2, norm_g, w_out, w_up, w_down, final_g, *, final):
    bsz, s, d = x.shape
    da = a.shape[-1]
    f = w_up.shape[1]
    tm = _tile(s, 512)
    kern = functools.partial(_out_mlp_kernel, fc=_tile(f, 1024), final=final)
    vec = pl.BlockSpec((1, 1, d), lambda b, i: (b, 0, 0))
    return pl.pallas_call(
        kern,
        grid=(bsz, s // tm),
        in_specs=[
            pl.BlockSpec((1, tm, d), lambda b, i: (b, i, 0)),
            pl.BlockSpec((1, tm, da), lambda b, i: (b, i, 0)),
            vec, vec, vec, vec,
            _resident((1, d)),
            _resident((da, d)),
            _resident((d, f)),
            _resident((f, d)),
            _resident((1, d)),
        ],
        out_specs=pl.BlockSpec((1, tm, d), lambda b, i: (b, i, 0)),
        out_shape=jax.ShapeDtypeStruct((bsz, s, d), F32),
        compiler_params=_params(("arbitrary", "arbitrary")),
        name="out_mlp",
    )(x, a, gate1, shift2, scale2, gate2, norm_g.reshape(1, d), w_out, w_up, w_down, final_g.reshape(1, d))


def _lambda_init(layer_number):
    return 0.8 - 0.6 * math.exp(-0.3 * (layer_number - 1))


def kernel(x, c, positions, ada_w, ada_b, norm_mix_g, norm_mlp_g, mlp_w_up, mlp_w_down, mlstm_w_in, mlstm_b_gates, mlstm_head_g, mlstm_w_out, kv_norm_g, kv_ada_w, kv_ada_b, kv_w, diff_w_q, diff_lambda, diff_subln_g, diff_w_out, final_norm_g):
    bsz, s, d = x.shape
    depth = ada_w.shape[0]
    n_a = mlstm_w_in.shape[0]
    a_heads = mlstm_b_gates.shape[-1]
    a_v_cols = mlstm_w_out.shape[1]
    a_q_cols = (mlstm_w_in.shape[-1] - 2 * a_v_cols - 2 * a_heads) // 2
    a_dk, a_dv = a_q_cols // a_heads, a_v_cols // a_heads
    b_comp = diff_lambda.shape[-1]
    b_heads = diff_w_q.shape[-1] // (2 * b_comp)
    rope_dim = b_comp // 4

    mods = _mods(c, ada_w, ada_b).reshape(depth, bsz, 1, 6 * d)
    kv_mods = _mods(c, kv_ada_w[None], kv_ada_b[None]).reshape(bsz, 1, 2 * d)
    cos_t, sin_t = _rope_tables(positions, b_comp, rope_dim)

    k = v = None
    for layer in range(depth):
        m = mods[layer]
        sh1, sc1, g1, sh2, sc2, g2 = [m[:, :, j * d:(j + 1) * d] for j in range(6)]
        if layer < n_a:
            w_in = mlstm_w_in[layer]
            n_main = 2 * a_q_cols + 2 * a_v_cols
            w_gates_t = w_in[:, n_main:].T.astype(BF16)
            b_g = mlstm_b_gates[layer].reshape(2 * a_heads, 1)
            qkvo, gates = _mlstm_proj(x, norm_mix_g[layer], sh1, sc1, w_in[:, :n_main].astype(BF16), w_gates_t, b_g,
                                      q_cols=a_q_cols, v_cols=a_v_cols, k_scale=a_dk ** -0.5, heads=a_heads,
                                      chunk=MLSTM_CHUNK)
            mix = _mlstm(qkvo, gates, mlstm_head_g[layer], heads=a_heads, dk=a_dk, dv=a_dv, chunk=MLSTM_CHUNK)
            w_o = mlstm_w_out[layer]
        else:
            j = layer - n_a
            (q,) = _rope_proj(x, norm_mix_g[layer], sh1, sc1, diff_w_q[j].astype(BF16), cos_t, sin_t,
                              n_out=1, rope_outs=1, out_scale=b_comp ** -0.5, half=rope_dim // 2)
            mix = _attention(q, k, v, diff_lambda[j], diff_subln_g[j], heads=b_heads, lam_init=_lambda_init(layer + 1))
            w_o = diff_w_out[j]
        x = _out_mlp(x, mix, g1, sh2, sc2, g2, norm_mlp_g[layer], w_o.astype(BF16), mlp_w_up[layer].astype(BF16),
                     mlp_w_down[layer].astype(BF16), final_norm_g, final=(layer == depth - 1))
        if layer == n_a - 1:
            k, v = _rope_proj(x, kv_norm_g, kv_mods[:, :, :d], kv_mods[:, :, d:], kv_w.astype(BF16), cos_t, sin_t,
                              n_out=2, rope_outs=1, out_scale=1.0, half=rope_dim // 2)
    return x
```

```python
import functools
import math

import jax
import jax.numpy as jnp
from jax import lax
from jax.experimental import pallas as pl
from jax.experimental.pallas import tpu as pltpu

F32 = jnp.float32
BF16 = jnp.bfloat16

NORM_EPS = 1e-6
SUBLN_EPS = 1e-5
ROPE_THETA = 500000.0
MLSTM_CHUNK = 64

V7X_VMEM_LIMIT_BYTES = 52 * 1024 * 1024
LANES = 128
BF16_SUBLANES = 16
LOG2E = math.log2(math.e)

_NT = (((1,), (1,)), ((), ()))
_TN = (((0,), (0,)), ((), ()))


def _tile(n, pref):
    t = min(n, pref)
    assert n % t == 0, (n, t)
    return t


def _params(sem):
    return pltpu.CompilerParams(dimension_semantics=sem, vmem_limit_bytes=V7X_VMEM_LIMIT_BYTES)


def _resident(shape):
    nd = len(shape)
    return pl.BlockSpec(shape, lambda *_: (0,) * nd, pipeline_mode=pl.Buffered(1))


def _rms(x, eps):
    return x * lax.rsqrt(jnp.mean(x * x, axis=-1, keepdims=True) + eps)


def _norm_mod(x, g, shift, scale):
    return (_rms(x, NORM_EPS) * g) * (1.0 + scale) + shift


def _mods_kernel(c_ref, w_ref, b_ref, o_ref):
    c = c_ref[...]
    act = (c * jax.nn.sigmoid(c)).astype(BF16)
    o_ref[0] = jnp.dot(act, w_ref[0].astype(BF16), preferred_element_type=F32) + b_ref[0]


def _mods(c, w, b):
    nl, d, n = w.shape
    bsz = c.shape[0]
    tn = _tile(n, 1024)
    return pl.pallas_call(
        _mods_kernel,
        grid=(nl, n // tn),
        in_specs=[
            pl.BlockSpec((bsz, d), lambda l, j: (0, 0)),
            pl.BlockSpec((1, d, tn), lambda l, j: (l, 0, j)),
            pl.BlockSpec((1, 1, tn), lambda l, j: (l, 0, j)),
        ],
        out_specs=pl.BlockSpec((1, bsz, tn), lambda l, j: (l, 0, j)),
        out_shape=jax.ShapeDtypeStruct((nl, bsz, n), F32),
        compiler_params=_params(("arbitrary", "arbitrary")),
        name="mods",
    )(c, w, b.reshape(nl, 1, n))


def _rope_table_kernel(pos_ref, cos_ref, sin_ref, *, comp_dim, rope_dim):
    half = rope_dim // 2
    pos = pos_ref[...].astype(F32)
    lane = lax.broadcasted_iota(jnp.int32, (1, LANES), 1) % comp_dim
    fidx = (lane % half).astype(F32)
    inv_freq = jnp.exp(fidx * (-2.0 * math.log(ROPE_THETA) / rope_dim))
    ang = pos * inv_freq
    rot = lane < rope_dim
    cos_ref[...] = jnp.where(rot, jnp.cos(ang), 1.0)
    sin_ref[...] = jnp.where(rot, jnp.where(lane < half, -jnp.sin(ang), jnp.sin(ang)), 0.0)


def _rope_tables(positions, comp_dim, rope_dim):
    t = positions.size
    tm = _tile(t, 2048)
    return pl.pallas_call(
        functools.partial(_rope_table_kernel, comp_dim=comp_dim, rope_dim=rope_dim),
        grid=(t // tm,),
        in_specs=[pl.BlockSpec((tm, 1), lambda i: (i, 0))],
        out_specs=[pl.BlockSpec((tm, LANES), lambda i: (i, 0))] * 2,
        out_shape=[jax.ShapeDtypeStruct((t, LANES), F32)] * 2,
        compiler_params=_params(("arbitrary",)),
        name="rope_tables",
    )(positions.reshape(t, 1))


def _apply_rope(r, cos_t, sin_t, half):
    lane = lax.broadcasted_iota(jnp.int32, cos_t.shape, 1)
    first = (lane % (LANES // 2)) < half
    outs = []
    for j in range(r.shape[1] // LANES):
        blk = r[:, j * LANES:(j + 1) * LANES]
        partner = jnp.where(first, pltpu.roll(blk, LANES - half, axis=1), pltpu.roll(blk, half, axis=1))
        outs.append(blk * cos_t + partner * sin_t)
    return outs


def _rope_proj_kernel(x_ref, g_ref, sh_ref, sc_ref, w_ref, cos_ref, sin_ref, *rest, half, out_scale, nc, with_vt):
    y = _norm_mod(x_ref[0], g_ref[...], sh_ref[0], sc_ref[0]).astype(BF16)
    cos_t = cos_ref[...]
    sin_t = sin_ref[...]
    o_ref = rest[-2] if with_vt else rest[-1]
    for c in range(o_ref.shape[-1] // nc):
        r = jnp.dot(y, w_ref[:, c * nc:(c + 1) * nc], preferred_element_type=F32)
        for j, blk in enumerate(_apply_rope(r, cos_t, sin_t, half)):
            if out_scale != 1.0:
                blk = blk * out_scale
            o_ref[0, :, c * nc + j * LANES:c * nc + (j + 1) * LANES] = blk.astype(o_ref.dtype)
    if with_vt:
        wvt_ref, vt_ref = rest[0], rest[-1]
        for c in range(vt_ref.shape[1] // nc):
            r = lax.dot_general(wvt_ref[c * nc:(c + 1) * nc, :], y, _NT, preferred_element_type=F32)
            vt_ref[0, c * nc:(c + 1) * nc, :] = r.astype(vt_ref.dtype)


def _rope_proj(x, g, shift, scale, w, cos_t, sin_t, w_vt=None, *, out_scale, half):
    bsz, s, d = x.shape
    n = w.shape[1]
    tm = _tile(s, 512)
    nt = s // tm
    with_vt = w_vt is not None
    kern = functools.partial(_rope_proj_kernel, half=half, out_scale=out_scale, nc=_tile(n, 512), with_vt=with_vt)
    vec = pl.BlockSpec((1, 1, d), lambda b, i: (b, 0, 0))
    tab = pl.BlockSpec((tm, LANES), lambda b, i: (b * nt + i, 0))
    in_specs = [pl.BlockSpec((1, tm, d), lambda b, i: (b, i, 0)), _resident((1, d)), vec, vec, _resident((d, n)), tab, tab]
    out_specs = [pl.BlockSpec((1, tm, n), lambda b, i: (b, i, 0))]
    out_shape = [jax.ShapeDtypeStruct((bsz, s, n), BF16)]
    args = [x, g.reshape(1, d), shift, scale, w, cos_t, sin_t]
    if with_vt:
        nv = w_vt.shape[0]
        in_specs.append(_resident((nv, d)))
        out_specs.append(pl.BlockSpec((1, nv, tm), lambda b, i: (b, 0, i)))
        out_shape.append(jax.ShapeDtypeStruct((bsz, nv, s), BF16))
        args.append(w_vt)
    return pl.pallas_call(
        kern,
        grid=(bsz, nt),
        in_specs=in_specs,
        out_specs=out_specs,
        out_shape=out_shape,
        compiler_params=_params(("arbitrary", "arbitrary")),
        name="rope_proj",
    )(*args)


def _mlstm_proj_kernel(x_ref, g_ref, sh_ref, sc_ref, w_ref, wg_ref, bg_ref, qkvo_ref, gates_ref, *, q_cols, v_cols, k_scale, heads, nc, chunk):
    y = _norm_mod(x_ref[0], g_ref[...], sh_ref[0], sc_ref[0]).astype(BF16)
    n = w_ref.shape[1]
    for c in range(n // nc):
        col = c * nc
        r = jnp.dot(y, w_ref[:, col:col + nc], preferred_element_type=F32)
        if q_cols <= col < 2 * q_cols:
            r = r * k_scale
        elif col >= 2 * q_cols + v_cols:
            r = jax.nn.sigmoid(r)
        qkvo_ref[0, :, col:col + nc] = r.astype(BF16)
    gt = lax.dot_general(wg_ref[...], y, _NT, preferred_element_type=F32) + bg_ref[...]
    row = lax.broadcasted_iota(jnp.int32, gt.shape, 0)
    logf = jnp.minimum(gt, 0.0) - jnp.log1p(jnp.exp(-jnp.abs(gt)))
    gt = jnp.where(row < heads, gt, logf)
    for c in range(gt.shape[1] // chunk):
        gates_ref[0, c] = gt[:, c * chunk:(c + 1) * chunk]


def _mlstm_proj(x, g, shift, scale, w_main, w_gates_t, b_gates, *, q_cols, v_cols, k_scale, heads, chunk):
    bsz, s, d = x.shape
    n = w_main.shape[1]
    tm = _tile(s, 512)
    kern = functools.partial(_mlstm_proj_kernel, q_cols=q_cols, v_cols=v_cols, k_scale=k_scale, heads=heads,
                             nc=_tile(q_cols, 512), chunk=chunk)
    vec = pl.BlockSpec((1, 1, d), lambda b, i: (b, 0, 0))
    return pl.pallas_call(
        kern,
        grid=(bsz, s // tm),
        in_specs=[
            pl.BlockSpec((1, tm, d), lambda b, i: (b, i, 0)),
            _resident((1, d)), vec, vec,
            _resident((d, n)),
            _resident((2 * heads, d)),
            _resident((2 * heads, 1)),
        ],
        out_specs=[
            pl.BlockSpec((1, tm, n), lambda b, i: (b, i, 0)),
            pl.BlockSpec((1, tm // chunk, 2 * heads, chunk), lambda b, i: (b, i, 0, 0)),
        ],
        out_shape=[
            jax.ShapeDtypeStruct((bsz, s, n), BF16),
            jax.ShapeDtypeStruct((bsz, s // chunk, 2 * heads, chunk), F32),
        ],
        compiler_params=_params(("arbitrary", "arbitrary")),
        name="mlstm_proj",
    )(x, g.reshape(1, d), shift, scale, w_main, w_gates_t, b_gates)


def _mlstm_kernel(q_ref, k_ref, v_ref, og_ref, gates_ref, hg_ref, out_ref, c_sc, n_sc, m_sc, *, heads, dk, dv, chunk):
    @pl.when(pl.program_id(1) == 0)
    def _():
        c_sc[...] = jnp.zeros_like(c_sc)
        n_sc[...] = jnp.zeros_like(n_sc)
        m_sc[...] = jnp.zeros_like(m_sc)

    L = chunk
    row = lax.broadcasted_iota(jnp.int32, (L, L), 0)
    col = lax.broadcasted_iota(jnp.int32, (L, L), 1)
    causal = col <= row
    upper = (row <= col).astype(F32)
    sel = jnp.concatenate([causal.astype(F32), (row == col).astype(F32)], axis=0)
    hi = lax.Precision.HIGHEST

    def body(c, carry):
        r0 = pl.multiple_of(c * L, L)
        gr = gates_ref[0, c]
        cum = jnp.dot(gr, upper, precision=hi, preferred_element_type=F32)
        cols = lax.dot_general(sel, gr, _NT, precision=hi, preferred_element_type=F32)
        for h in range(heads):
            i_row = gr[h:h + 1, :]
            g_row = cum[heads + h:heads + h + 1, :]
            g_col = cols[:L, heads + h:heads + h + 1]
            i_col = cols[L:, h:h + 1]
            g_tot = g_row[:, L - 1:L]
            m_old = m_sc[h]
            qh = q_ref[0, pl.ds(r0, L), h * dk:(h + 1) * dk]
            kh = k_ref[0, pl.ds(r0, L), h * dk:(h + 1) * dk]
            vh = v_ref[0, pl.ds(r0, L), h * dv:(h + 1) * dv]
            oh = og_ref[0, pl.ds(r0, L), h * dv:(h + 1) * dv]
            c_old = c_sc[h]
            n_old = n_sc[h]

            dm = jnp.where(causal, g_col - g_row + i_row, -jnp.inf)
            inter = g_col + m_old
            m_row = jnp.maximum(inter, jnp.max(dm, axis=-1, keepdims=True))
            w_intra = jnp.exp(dm - m_row)
            w_inter = jnp.exp(inter - m_row)
            s = lax.dot_general(qh, kh, _NT, preferred_element_type=F32) * w_intra
            num = jnp.dot(s.astype(BF16), vh, preferred_element_type=F32)
            num = num + w_inter * jnp.dot(qh, c_old.astype(BF16), preferred_element_type=F32)
            qn = jnp.sum(qh.astype(F32) * n_old.astype(BF16).astype(F32), axis=-1, keepdims=True)
            den = jnp.sum(s, axis=-1, keepdims=True) + w_inter * qn
            hh = num / jnp.maximum(jnp.abs(den), jnp.exp(-m_row))
            hh = _rms(hh, NORM_EPS) * hg_ref[:, h * dv:(h + 1) * dv]
            out_ref[0, pl.ds(r0, L), h * dv:(h + 1) * dv] = (hh * oh.astype(F32)).astype(out_ref.dtype)

            a_row = g_tot - g_row + i_row
            a_col = g_tot - g_col + i_col
            m_new = jnp.maximum(g_tot + m_old, jnp.max(a_row, axis=-1, keepdims=True))
            w_old = jnp.exp(g_tot + m_old - m_new)
            kw = kh.astype(F32) * jnp.exp(a_col - m_new)
            c_sc[h] = w_old * c_old + lax.dot_general(kw.astype(BF16), vh, _TN, preferred_element_type=F32)
            n_sc[h] = w_old * n_old + jnp.sum(kw, axis=0, keepdims=True)
            m_sc[h] = m_new
        return carry

    lax.fori_loop(0, q_ref.shape[1] // L, body, 0)


def _mlstm(qkvo, gates, head_g, *, heads, dk, dv, chunk):
    bsz, s, _ = qkvo.shape
    ts = _tile(s, 512)
    qc, vc = heads * dk, heads * dv
    kern = functools.partial(_mlstm_kernel, heads=heads, dk=dk, dv=dv, chunk=chunk)
    return pl.pallas_call(
        kern,
        grid=(bsz, s // ts),
        in_specs=[
            pl.BlockSpec((1, ts, qc), lambda b, i: (b, i, 0)),
            pl.BlockSpec((1, ts, qc), lambda b, i: (b, i, 1)),
            pl.BlockSpec((1, ts, vc), lambda b, i: (b, i, (2 * qc) // vc)),
            pl.BlockSpec((1, ts, vc), lambda b, i: (b, i, (2 * qc) // vc + 1)),
            pl.BlockSpec((1, ts // chunk, 2 * heads, chunk), lambda b, i: (b, i, 0, 0)),
            _resident((1, vc)),
        ],
        out_specs=pl.BlockSpec((1, ts, vc), lambda b, i: (b, i, 0)),
        out_shape=jax.ShapeDtypeStruct((bsz, s, vc), BF16),
        scratch_shapes=[
            pltpu.VMEM((heads, dk, dv), F32),
            pltpu.VMEM((heads, 1, dk), F32),
            pltpu.VMEM((heads, 1, 1), F32),
        ],
        compiler_params=_params(("arbitrary", "arbitrary")),
        name="mlstm",
    )(qkvo, qkvo, qkvo, qkvo, gates, head_g.reshape(1, vc))


def _attn_kernel(q_ref, k_ref, vt_ref, lam_ref, g_ref, o_ref, m_sc, acc_sc, *, tq, tk, hd, nh, ahead, lam_init):
    lp = lam_ref[...]
    lam = (jnp.exp(jnp.sum(lp[0:1] * lp[1:2], axis=-1, keepdims=True))
           - jnp.exp(jnp.sum(lp[2:3] * lp[3:4], axis=-1, keepdims=True)) + lam_init)
    g_col = g_ref[...] * (1.0 - lam_init)
    per_q = tq // tk
    comp_dim = hd // 2

    def q_tile(qi, carry):
        q0 = pl.multiple_of(qi * tq, tq)
        qqs = []
        for h in range(nh):
            q = q_ref[0, pl.ds(q0, tq), h * hd:(h + 1) * hd]
            lane = lax.broadcasted_iota(jnp.int32, q.shape, 1)
            zero = jnp.zeros_like(q)
            qqs.append(jnp.concatenate([jnp.where(lane < comp_dim, q, zero), jnp.where(lane >= comp_dim, q, zero)], axis=0))
        m_sc[...] = jnp.full_like(m_sc, -jnp.inf)
        acc_sc[...] = jnp.zeros_like(acc_sc)
        ones = jnp.ones((acc_sc.shape[1] - hd, tk), BF16)

        def step(j, masked):
            k0 = pl.multiple_of(j * tk, tk)

            def scores(h):
                kk = k_ref[0, pl.ds(k0, tk), h * hd:(h + 1) * hd]
                return lax.dot_general(kk, qqs[h], _NT, preferred_element_type=F32)

            pending = [scores(h) for h in range(min(ahead, nh))]
            for h in range(nh):
                s = pending.pop(0)
                if h + ahead < nh:
                    pending.append(scores(h + ahead))
                vt = jnp.concatenate([vt_ref[0, h * hd:(h + 1) * hd, pl.ds(k0, tk)], ones], axis=0)
                if masked:
                    kpos = lax.broadcasted_iota(jnp.int32, s.shape, 0) + k0
                    qpos = lax.broadcasted_iota(jnp.int32, s.shape, 1) % tq + q0
                    s = jnp.where(kpos <= qpos, s, -jnp.inf)
                m_old = m_sc[h]
                m_new = jnp.maximum(m_old, jnp.max(s, axis=0, keepdims=True))
                alpha = jnp.exp2(m_old - m_new)
                p = jnp.exp2(s - m_new).astype(BF16)
                acc_sc[h] = alpha * acc_sc[h] + jnp.dot(vt, p, preferred_element_type=F32)
                m_sc[h] = m_new

        def full_body(j, c):
            step(j, False)
            return c

        lax.fori_loop(0, qi * per_q, full_body, 0)
        for d in range(per_q):
            step(qi * per_q + d, True)

        for h in range(nh):
            acc = acc_sc[h]
            acc = acc[:hd] / acc[hd:hd + 1]
            o = acc[:, :tq] - lam * acc[:, tq:]
            o = o * lax.rsqrt(jnp.mean(o * o, axis=0, keepdims=True) + SUBLN_EPS) * g_col
            o_ref[0, pl.ds(q0, tq), h * hd:(h + 1) * hd] = o.T.astype(o_ref.dtype)
        return carry

    lax.fori_loop(0, q_ref.shape[1] // tq, q_tile, 0)


def _attention(q, k, vt, lam_params, subln_g, *, heads, lam_init):
    bsz, s, d = q.shape
    hd = d // heads
    tq = _tile(s, 256)
    tk = tq
    nh = _tile(heads, 4)
    kern = functools.partial(_attn_kernel, tq=tq, tk=tk, hd=hd, nh=nh, ahead=2, lam_init=lam_init)
    return pl.pallas_call(
        kern,
        grid=(bsz, heads // nh),
        in_specs=[
            pl.BlockSpec((1, s, nh * hd), lambda b, h: (b, 0, h)),
            pl.BlockSpec((1, s, nh * hd), lambda b, h: (b, 0, h)),
            pl.BlockSpec((1, nh * hd, s), lambda b, h: (b, h, 0)),
            _resident(lam_params.shape),
            _resident((hd, 1)),
        ],
        out_specs=pl.BlockSpec((1, s, nh * hd), lambda b, h: (b, 0, h)),
        out_shape=jax.ShapeDtypeStruct((bsz, s, d), BF16),
        scratch_shapes=[
            pltpu.VMEM((nh, 1, 2 * tq), F32),
            pltpu.VMEM((nh, hd + BF16_SUBLANES, 2 * tq), F32),
        ],
        compiler_params=_params(("arbitrary", "arbitrary")),
        name="diff_attn",
    )(q, k, vt, lam_params, subln_g.reshape(hd, 1))


def _out_mlp_kernel(x_ref, a_ref, g1_ref, sh_ref, sc_ref, g2_ref, ng_ref, wo_ref, wu_ref, wd_ref, fg_ref, o_ref, *, fc, final):
    x1 = x_ref[0] + g1_ref[0] * jnp.dot(a_ref[0], wo_ref[...], preferred_element_type=F32)
    y = _norm_mod(x1, ng_ref[...], sh_ref[0], sc_ref[0]).astype(BF16)
    acc = jnp.zeros_like(x1)
    for c in range(wu_ref.shape[1] // fc):
        h = jnp.dot(y, wu_ref[:, c * fc:(c + 1) * fc], preferred_element_type=F32)
        h = jnp.square(jnp.maximum(h, 0.0)).astype(BF16)
        acc = acc + jnp.dot(h, wd_ref[c * fc:(c + 1) * fc, :], preferred_element_type=F32)
    x2 = x1 + g2_ref[0] * acc
    if final:
        x2 = _rms(x2, NORM_EPS) * fg_ref[...]
    o_ref[0] = x2


def _out_mlp(x, a, gate1, shift2, scale2, gate2, norm_g, w_out, w_up, w_down, final_g, *, final):
    bsz, s, d = x.shape
    da = a.shape[-1]
    f = w_up.shape[1]
    tm = _tile(s, 512)
    kern = functools.partial(_out_mlp_kernel, fc=_tile(f, 1024), final=final)
    vec = pl.BlockSpec((1, 1, d), lambda b, i: (b, 0, 0))
    return pl.pallas_call(
        kern,
        grid=(bsz, s // tm),
        in_specs=[
            pl.BlockSpec((1, tm, d), lambda b, i: (b, i, 0)),
            pl.BlockSpec((1, tm, da), lambda b, i: (b, i, 0)),
            vec, vec, vec, vec,
            _resident((1, d)),
            _resident((da, d)),
            _resident((d, f)),
            _resident((f, d)),
            _resident((1, d)),
        ],
        out_specs=pl.BlockSpec((1, tm, d), lambda b, i: (b, i, 0)),
        out_shape=jax.ShapeDtypeStruct((bsz, s, d), F32),
        compiler_params=_params(("arbitrary", "arbitrary")),
        name="out_mlp",
    )(x, a, gate1, shift2, scale2, gate2, norm_g.reshape(1, d), w_out, w_up, w_down, final_g.reshape(1, d))


def _lambda_init(layer_number):
    return 0.8 - 0.6 * math.exp(-0.3 * (layer_number - 1))


def kernel(x, c, positions, ada_w, ada_b, norm_mix_g, norm_mlp_g, mlp_w_up, mlp_w_down, mlstm_w_in, mlstm_b_gates, mlstm_head_g, mlstm_w_out, kv_norm_g, kv_ada_w, kv_ada_b, kv_w, diff_w_q, diff_lambda, diff_subln_g, diff_w_out, final_norm_g):
    bsz, s, d = x.shape
    depth = ada_w.shape[0]
    n_a = mlstm_w_in.shape[0]
    a_heads = mlstm_b_gates.shape[-1]
    a_v_cols = mlstm_w_out.shape[1]
    a_q_cols = (mlstm_w_in.shape[-1] - 2 * a_v_cols - 2 * a_heads) // 2
    a_dk, a_dv = a_q_cols // a_heads, a_v_cols // a_heads
    b_comp = diff_lambda.shape[-1]
    b_heads = diff_w_q.shape[-1] // (2 * b_comp)
    rope_dim = b_comp // 4

    mods = _mods(c, ada_w, ada_b).reshape(depth, bsz, 1, 6 * d)
    kv_mods = _mods(c, kv_ada_w[None], kv_ada_b[None]).reshape(bsz, 1, 2 * d)
    cos_t, sin_t = _rope_tables(positions, b_comp, rope_dim)

    k = vt = None
    for layer in range(depth):
        m = mods[layer]
        sh1, sc1, g1, sh2, sc2, g2 = [m[:, :, j * d:(j + 1) * d] for j in range(6)]
        if layer < n_a:
            w_in = mlstm_w_in[layer]
            n_main = 2 * a_q_cols + 2 * a_v_cols
            w_gates_t = w_in[:, n_main:].T.astype(BF16)
            b_g = mlstm_b_gates[layer].reshape(2 * a_heads, 1)
            qkvo, gates = _mlstm_proj(x, norm_mix_g[layer], sh1, sc1, w_in[:, :n_main].astype(BF16), w_gates_t, b_g,
                                      q_cols=a_q_cols, v_cols=a_v_cols, k_scale=a_dk ** -0.5, heads=a_heads,
                                      chunk=MLSTM_CHUNK)
            mix = _mlstm(qkvo, gates, mlstm_head_g[layer], heads=a_heads, dk=a_dk, dv=a_dv, chunk=MLSTM_CHUNK)
            w_o = mlstm_w_out[layer]
        else:
            j = layer - n_a
            (q,) = _rope_proj(x, norm_mix_g[layer], sh1, sc1, diff_w_q[j].astype(BF16), cos_t, sin_t,
                              out_scale=b_comp ** -0.5 * LOG2E, half=rope_dim // 2)
            mix = _attention(q, k, vt, diff_lambda[j], diff_subln_g[j], heads=b_heads, lam_init=_lambda_init(layer + 1))
            w_o = diff_w_out[j]
        x = _out_mlp(x, mix, g1, sh2, sc2, g2, norm_mlp_g[layer], w_o.astype(BF16), mlp_w_up[layer].astype(BF16),
                     mlp_w_down[layer].astype(BF16), final_norm_g, final=(layer == depth - 1))
        if layer == n_a - 1:
            n_k = b_heads * 2 * b_comp
            k, vt = _rope_proj(x, kv_norm_g, kv_mods[:, :, :d], kv_mods[:, :, d:], kv_w[:, :n_k].astype(BF16), cos_t, sin_t,
                               kv_w[:, n_k:].T.astype(BF16), out_scale=1.0, half=rope_dim // 2)
    return x
```

```python
import functools
import math

import jax
import jax.numpy as jnp
from jax import lax
from jax.experimental import pallas as pl
from jax.experimental.pallas import tpu as pltpu

F32 = jnp.float32
BF16 = jnp.bfloat16

NORM_EPS = 1e-6
SUBLN_EPS = 1e-5
ROPE_THETA = 500000.0
MLSTM_CHUNK = 256

V7X_VMEM_LIMIT_BYTES = 52 * 1024 * 1024
LANES = 128
BF16_SUBLANES = 16
LOG2E = math.log2(math.e)

_NT = (((1,), (1,)), ((), ()))
_TN = (((0,), (0,)), ((), ()))


def _tile(n, pref):
    t = min(n, pref)
    assert n % t == 0, (n, t)
    return t


def _params(sem):
    return pltpu.CompilerParams(dimension_semantics=sem, vmem_limit_bytes=V7X_VMEM_LIMIT_BYTES)


def _resident(shape):
    nd = len(shape)
    return pl.BlockSpec(shape, lambda *_: (0,) * nd, pipeline_mode=pl.Buffered(1))


def _rms(x, eps):
    return x * lax.rsqrt(jnp.mean(x * x, axis=-1, keepdims=True) + eps)


def _norm_mod(x, g, shift, scale):
    return (_rms(x, NORM_EPS) * g) * (1.0 + scale) + shift


def _mods_kernel(c_ref, w_ref, b_ref, o_ref):
    c = c_ref[...]
    act = (c * jax.nn.sigmoid(c)).astype(BF16)
    o_ref[0] = jnp.dot(act, w_ref[0].astype(BF16), preferred_element_type=F32) + b_ref[0]


def _mods(c, w, b):
    nl, d, n = w.shape
    bsz = c.shape[0]
    tn = _tile(n, 1024)
    return pl.pallas_call(
        _mods_kernel,
        grid=(nl, n // tn),
        in_specs=[
            pl.BlockSpec((bsz, d), lambda l, j: (0, 0)),
            pl.BlockSpec((1, d, tn), lambda l, j: (l, 0, j)),
            pl.BlockSpec((1, 1, tn), lambda l, j: (l, 0, j)),
        ],
        out_specs=pl.BlockSpec((1, bsz, tn), lambda l, j: (l, 0, j)),
        out_shape=jax.ShapeDtypeStruct((nl, bsz, n), F32),
        compiler_params=_params(("arbitrary", "arbitrary")),
        name="mods",
    )(c, w, b.reshape(nl, 1, n))


def _rope_table_kernel(pos_ref, cos_ref, sin_ref, *, comp_dim, rope_dim):
    half = rope_dim // 2
    pos = pos_ref[...].astype(F32)
    lane = lax.broadcasted_iota(jnp.int32, (1, LANES), 1) % comp_dim
    fidx = (lane % half).astype(F32)
    inv_freq = jnp.exp(fidx * (-2.0 * math.log(ROPE_THETA) / rope_dim))
    ang = pos * inv_freq
    rot = lane < rope_dim
    cos_ref[...] = jnp.where(rot, jnp.cos(ang), 1.0)
    sin_ref[...] = jnp.where(rot, jnp.where(lane < half, -jnp.sin(ang), jnp.sin(ang)), 0.0)


def _rope_tables(positions, comp_dim, rope_dim):
    t = positions.size
    tm = _tile(t, 2048)
    return pl.pallas_call(
        functools.partial(_rope_table_kernel, comp_dim=comp_dim, rope_dim=rope_dim),
        grid=(t // tm,),
        in_specs=[pl.BlockSpec((tm, 1), lambda i: (i, 0))],
        out_specs=[pl.BlockSpec((tm, LANES), lambda i: (i, 0))] * 2,
        out_shape=[jax.ShapeDtypeStruct((t, LANES), F32)] * 2,
        compiler_params=_params(("arbitrary",)),
        name="rope_tables",
    )(positions.reshape(t, 1))


def _apply_rope(r, cos_t, sin_t, half):
    lane = lax.broadcasted_iota(jnp.int32, cos_t.shape, 1)
    first = (lane % (LANES // 2)) < half
    outs = []
    for j in range(r.shape[1] // LANES):
        blk = r[:, j * LANES:(j + 1) * LANES]
        partner = jnp.where(first, pltpu.roll(blk, LANES - half, axis=1), pltpu.roll(blk, half, axis=1))
        outs.append(blk * cos_t + partner * sin_t)
    return outs


def _rope_proj_kernel(x_ref, g_ref, sh_ref, sc_ref, w_ref, cos_ref, sin_ref, *rest, half, out_scale, nc, with_vt):
    y = _norm_mod(x_ref[0], g_ref[...], sh_ref[0], sc_ref[0]).astype(BF16)
    cos_t = cos_ref[...]
    sin_t = sin_ref[...]
    o_ref = rest[-2] if with_vt else rest[-1]
    for c in range(o_ref.shape[-1] // nc):
        r = jnp.dot(y, w_ref[:, c * nc:(c + 1) * nc], preferred_element_type=F32)
        for j, blk in enumerate(_apply_rope(r, cos_t, sin_t, half)):
            if out_scale != 1.0:
                blk = blk * out_scale
            o_ref[0, :, c * nc + j * LANES:c * nc + (j + 1) * LANES] = blk.astype(o_ref.dtype)
    if with_vt:
        wvt_ref, vt_ref = rest[0], rest[-1]
        for c in range(vt_ref.shape[1] // nc):
            r = lax.dot_general(wvt_ref[c * nc:(c + 1) * nc, :], y, _NT, preferred_element_type=F32)
            vt_ref[0, c * nc:(c + 1) * nc, :] = r.astype(vt_ref.dtype)


def _rope_proj(x, g, shift, scale, w, cos_t, sin_t, w_vt=None, *, out_scale, half):
    bsz, s, d = x.shape
    n = w.shape[1]
    tm = _tile(s, 512)
    nt = s // tm
    with_vt = w_vt is not None
    kern = functools.partial(_rope_proj_kernel, half=half, out_scale=out_scale, nc=_tile(n, 512), with_vt=with_vt)
    vec = pl.BlockSpec((1, 1, d), lambda b, i: (b, 0, 0))
    tab = pl.BlockSpec((tm, LANES), lambda b, i: (b * nt + i, 0))
    in_specs = [pl.BlockSpec((1, tm, d), lambda b, i: (b, i, 0)), _resident((1, d)), vec, vec, _resident((d, n)), tab, tab]
    out_specs = [pl.BlockSpec((1, tm, n), lambda b, i: (b, i, 0))]
    out_shape = [jax.ShapeDtypeStruct((bsz, s, n), BF16)]
    args = [x, g.reshape(1, d), shift, scale, w, cos_t, sin_t]
    if with_vt:
        nv = w_vt.shape[0]
        in_specs.append(_resident((nv, d)))
        out_specs.append(pl.BlockSpec((1, nv, tm), lambda b, i: (b, 0, i)))
        out_shape.append(jax.ShapeDtypeStruct((bsz, nv, s), BF16))
        args.append(w_vt)
    return pl.pallas_call(
        kern,
        grid=(bsz, nt),
        in_specs=in_specs,
        out_specs=out_specs,
        out_shape=out_shape,
        compiler_params=_params(("arbitrary", "arbitrary")),
        name="rope_proj",
    )(*args)


def _mlstm_proj_kernel(x_ref, g_ref, sh_ref, sc_ref, w_ref, wkt_ref, wg_ref, bg_ref, voq_ref, kt_ref, gates_ref, *, v_cols, k_scale, heads, nc, chunk):
    y = _norm_mod(x_ref[0], g_ref[...], sh_ref[0], sc_ref[0]).astype(BF16)
    for c in range(w_ref.shape[1] // nc):
        col = c * nc
        r = jnp.dot(y, w_ref[:, col:col + nc], preferred_element_type=F32)
        if v_cols <= col < 2 * v_cols:
            r = jax.nn.sigmoid(r)
        voq_ref[0, :, col:col + nc] = r.astype(BF16)
    kt = lax.dot_general(wkt_ref[...], y, _NT, preferred_element_type=F32) * k_scale
    kt_ref[0] = kt.astype(BF16)
    gt = lax.dot_general(wg_ref[...], y, _NT, preferred_element_type=F32) + bg_ref[...]
    row = lax.broadcasted_iota(jnp.int32, gt.shape, 0)
    logf = jnp.minimum(gt, 0.0) - jnp.log1p(jnp.exp(-jnp.abs(gt)))
    gt = jnp.where(row < heads, gt, logf)
    for c in range(gt.shape[1] // chunk):
        gates_ref[0, c] = gt[:, c * chunk:(c + 1) * chunk]


def _mlstm_proj(x, g, shift, scale, w_voq, w_kt, w_gates_t, b_gates, *, v_cols, k_scale, heads, chunk):
    bsz, s, d = x.shape
    n = w_voq.shape[1]
    q_cols = w_kt.shape[0]
    tm = _tile(s, 512)
    kern = functools.partial(_mlstm_proj_kernel, v_cols=v_cols, k_scale=k_scale, heads=heads,
                             nc=_tile(q_cols, 512), chunk=chunk)
    vec = pl.BlockSpec((1, 1, d), lambda b, i: (b, 0, 0))
    return pl.pallas_call(
        kern,
        grid=(bsz, s // tm),
        in_specs=[
            pl.BlockSpec((1, tm, d), lambda b, i: (b, i, 0)),
            _resident((1, d)), vec, vec,
            _resident((d, n)),
            _resident((q_cols, d)),
            _resident((2 * heads, d)),
            _resident((2 * heads, 1)),
        ],
        out_specs=[
            pl.BlockSpec((1, tm, n), lambda b, i: (b, i, 0)),
            pl.BlockSpec((1, q_cols, tm), lambda b, i: (b, 0, i)),
            pl.BlockSpec((1, tm // chunk, 2 * heads, chunk), lambda b, i: (b, i, 0, 0)),
        ],
        out_shape=[
            jax.ShapeDtypeStruct((bsz, s, n), BF16),
            jax.ShapeDtypeStruct((bsz, q_cols, s), BF16),
            jax.ShapeDtypeStruct((bsz, s // chunk, 2 * heads, chunk), F32),
        ],
        compiler_params=_params(("arbitrary", "arbitrary")),
        name="mlstm_proj",
    )(x, g.reshape(1, d), shift, scale, w_voq, w_kt, w_gates_t, b_gates)


def _prefix_max(x):
    n = x.shape[1]
    lane = lax.broadcasted_iota(jnp.int32, x.shape, 1)
    shift = 1
    while shift < n:
        x = jnp.maximum(x, jnp.where(lane >= shift, pltpu.roll(x, shift, axis=1), -jnp.inf))
        shift *= 2
    return x


def _split3(x):
    hi = x.astype(BF16)
    r1 = x - hi.astype(F32)
    mid = r1.astype(BF16)
    lo = (r1 - mid.astype(F32)).astype(BF16)
    return hi, mid, lo


def _mlstm_kernel(v_ref, og_ref, q_ref, kt_ref, gates_ref, hg_ref, out_ref, c_sc, n_sc, m_sc, *, heads, dk, dv):
    H = heads
    L = q_ref.shape[1]

    @pl.when(pl.program_id(1) == 0)
    def _():
        c_sc[...] = jnp.zeros_like(c_sc)
        n_sc[...] = jnp.zeros_like(n_sc)
        m_sc[...] = jnp.zeros_like(m_sc)

    row = lax.broadcasted_iota(jnp.int32, (L, L), 0)
    col = lax.broadcasted_iota(jnp.int32, (L, L), 1)
    causal = col <= row
    upper = (row <= col).astype(F32)

    gr = gates_ref[0, 0]
    g = jnp.dot(gr, upper, precision=lax.Precision.HIGHEST, preferred_element_type=F32)[H:]
    b = gr[:H] - g
    cm = _prefix_max(b)
    rows = jnp.concatenate([cm, g], axis=0)
    stacked = jnp.concatenate(list(_split3(rows)) + [jnp.zeros_like(rows, dtype=BF16)], axis=0)
    sel_r = lax.broadcasted_iota(jnp.int32, (8 * H, 2 * H * LANES), 0) % (2 * H)
    sel_c = lax.broadcasted_iota(jnp.int32, (8 * H, 2 * H * LANES), 1) // LANES
    sel = (sel_r == sel_c).astype(BF16)
    cols = lax.dot_general(stacked, sel, _TN, preferred_element_type=F32)
    ones = jnp.ones((L, LANES), BF16)
    rep_l = L // LANES
    rep_v = dv // LANES

    for h in range(H):
        cm_c = cols[:, h * LANES:(h + 1) * LANES]
        g_c = cols[:, (H + h) * LANES:(H + h + 1) * LANES]
        b_row = b[h:h + 1]
        g_tot = g[h:h + 1, L - 1:L]
        m_old = m_sc[h]
        m_c = jnp.maximum(cm_c, m_old)
        m_end = jnp.maximum(m_old, jnp.max(b_row, axis=-1, keepdims=True))
        qh = q_ref[0, :, h * dk:(h + 1) * dk]
        kth = kt_ref[0, h * dk:(h + 1) * dk, :]
        vh = v_ref[0, :, h * dv:(h + 1) * dv]
        oh = og_ref[0, :, h * dv:(h + 1) * dv]
        c_old = c_sc[h]
        n_old = n_sc[h]

        s = jnp.dot(qh, kth, preferred_element_type=F32)
        w = jnp.exp(jnp.where(causal, b_row - jnp.tile(m_c, (1, rep_l)), -jnp.inf))
        p = (s * w).astype(BF16)
        w_inter = jnp.exp(m_old - m_c)
        num = jnp.dot(p, vh, preferred_element_type=F32)
        num = num + jnp.tile(w_inter, (1, rep_v)) * jnp.dot(qh, c_old.astype(BF16), preferred_element_type=F32)
        den = jnp.dot(p, ones, preferred_element_type=F32)
        den = den + w_inter * jnp.dot(qh, n_old.astype(BF16), preferred_element_type=F32)
        inv = 1.0 / jnp.maximum(jnp.abs(den), jnp.exp(-(g_c + m_c)))
        hh = num * jnp.tile(inv, (1, rep_v))
        hh = _rms(hh, NORM_EPS) * hg_ref[:, h * dv:(h + 1) * dv]
        out_ref[0, :, h * dv:(h + 1) * dv] = (hh * oh.astype(F32)).astype(out_ref.dtype)

        w_old = jnp.exp(m_old - m_end)
        kw = (kth.astype(F32) * jnp.exp(b_row - m_end)).astype(BF16)
        c_sc[h] = w_old * c_old + jnp.dot(kw, vh, preferred_element_type=F32)
        n_sc[h] = w_old * n_old + jnp.dot(kw, ones, preferred_element_type=F32)
        m_sc[h] = g_tot + m_end


def _mlstm(voq, kt, gates, head_g, *, heads, dk, dv, chunk):
    bsz, s, _ = voq.shape
    qc, vc = heads * dk, heads * dv
    kern = functools.partial(_mlstm_kernel, heads=heads, dk=dk, dv=dv)
    return pl.pallas_call(
        kern,
        grid=(bsz, s // chunk),
        in_specs=[
            pl.BlockSpec((1, chunk, vc), lambda b, i: (b, i, 0)),
            pl.BlockSpec((1, chunk, vc), lambda b, i: (b, i, 1)),
            pl.BlockSpec((1, chunk, qc), lambda b, i: (b, i, (2 * vc) // qc)),
            pl.BlockSpec((1, qc, chunk), lambda b, i: (b, 0, i)),
            pl.BlockSpec((1, 1, 2 * heads, chunk), lambda b, i: (b, i, 0, 0)),
            _resident((1, vc)),
        ],
        out_specs=pl.BlockSpec((1, chunk, vc), lambda b, i: (b, i, 0)),
        out_shape=jax.ShapeDtypeStruct((bsz, s, vc), BF16),
        scratch_shapes=[
            pltpu.VMEM((heads, dk, dv), F32),
            pltpu.VMEM((heads, dk, LANES), F32),
            pltpu.VMEM((heads, 1, 1), F32),
        ],
        compiler_params=_params(("arbitrary", "arbitrary")),
        name="mlstm",
    )(voq, voq, voq, kt, gates, head_g.reshape(1, vc))


def _attn_kernel(q_ref, k_ref, vt_ref, lam_ref, g_ref, o_ref, m_sc, acc_sc, *, tq, tk, hd, nh, ahead, lam_init):
    lp = lam_ref[...]
    lam = (jnp.exp(jnp.sum(lp[0:1] * lp[1:2], axis=-1, keepdims=True))
           - jnp.exp(jnp.sum(lp[2:3] * lp[3:4], axis=-1, keepdims=True)) + lam_init)
    g_col = g_ref[...] * (1.0 - lam_init)
    per_q = tq // tk
    comp_dim = hd // 2

    def q_tile(qi, carry):
        q0 = pl.multiple_of(qi * tq, tq)
        qqs = []
        for h in range(nh):
            q = q_ref[0, pl.ds(q0, tq), h * hd:(h + 1) * hd]
            lane = lax.broadcasted_iota(jnp.int32, q.shape, 1)
            zero = jnp.zeros_like(q)
            qqs.append(jnp.concatenate([jnp.where(lane < comp_dim, q, zero), jnp.where(lane >= comp_dim, q, zero)], axis=0))
        m_sc[...] = jnp.full_like(m_sc, -jnp.inf)
        acc_sc[...] = jnp.zeros_like(acc_sc)
        ones = jnp.ones((acc_sc.shape[1] - hd, tk), BF16)

        def step(j, masked):
            k0 = pl.multiple_of(j * tk, tk)

            def scores(h):
                kk = k_ref[0, pl.ds(k0, tk), h * hd:(h + 1) * hd]
                return lax.dot_general(kk, qqs[h], _NT, preferred_element_type=F32)

            pending = [scores(h) for h in range(min(ahead, nh))]
            for h in range(nh):
                s = pending.pop(0)
                if h + ahead < nh:
                    pending.append(scores(h + ahead))
                vt = jnp.concatenate([vt_ref[0, h * hd:(h + 1) * hd, pl.ds(k0, tk)], ones], axis=0)
                if masked:
                    kpos = lax.broadcasted_iota(jnp.int32, s.shape, 0) + k0
                    qpos = lax.broadcasted_iota(jnp.int32, s.shape, 1) % tq + q0
                    s = jnp.where(kpos <= qpos, s, -jnp.inf)
                m_old = m_sc[h]
                m_new = jnp.maximum(m_old, jnp.max(s, axis=0, keepdims=True))
                alpha = jnp.exp2(m_old - m_new)
                p = jnp.exp2(s - m_new).astype(BF16)
                acc_sc[h] = alpha * acc_sc[h] + jnp.dot(vt, p, preferred_element_type=F32)
                m_sc[h] = m_new

        def full_body(j, c):
            step(j, False)
            return c

        lax.fori_loop(0, qi * per_q, full_body, 0)
        for d in range(per_q):
            step(qi * per_q + d, True)

        for h in range(nh):
            acc = acc_sc[h]
            acc = acc[:hd] / acc[hd:hd + 1]
            o = acc[:, :tq] - lam * acc[:, tq:]
            o = o * lax.rsqrt(jnp.mean(o * o, axis=0, keepdims=True) + SUBLN_EPS) * g_col
            o_ref[0, pl.ds(q0, tq), h * hd:(h + 1) * hd] = o.T.astype(o_ref.dtype)
        return carry

    lax.fori_loop(0, q_ref.shape[1] // tq, q_tile, 0)


def _attention(q, k, vt, lam_params, subln_g, *, heads, lam_init):
    bsz, s, d = q.shape
    hd = d // heads
    tq = _tile(s, 256)
    tk = tq
    nh = _tile(heads, 4)
    kern = functools.partial(_attn_kernel, tq=tq, tk=tk, hd=hd, nh=nh, ahead=2, lam_init=lam_init)
    return pl.pallas_call(
        kern,
        grid=(bsz, heads // nh),
        in_specs=[
            pl.BlockSpec((1, s, nh * hd), lambda b, h: (b, 0, h)),
            pl.BlockSpec((1, s, nh * hd), lambda b, h: (b, 0, h)),
            pl.BlockSpec((1, nh * hd, s), lambda b, h: (b, h, 0)),
            _resident(lam_params.shape),
            _resident((hd, 1)),
        ],
        out_specs=pl.BlockSpec((1, s, nh * hd), lambda b, h: (b, 0, h)),
        out_shape=jax.ShapeDtypeStruct((bsz, s, d), BF16),
        scratch_shapes=[
            pltpu.VMEM((nh, 1, 2 * tq), F32),
            pltpu.VMEM((nh, hd + BF16_SUBLANES, 2 * tq), F32),
        ],
        compiler_params=_params(("arbitrary", "arbitrary")),
        name="diff_attn",
    )(q, k, vt, lam_params, subln_g.reshape(hd, 1))


def _out_mlp_kernel(x_ref, a_ref, g1_ref, sh_ref, sc_ref, g2_ref, ng_ref, wo_ref, wu_ref, wd_ref, fg_ref, o_ref, *, fc, final):
    x1 = x_ref[0] + g1_ref[0] * jnp.dot(a_ref[0], wo_ref[...], preferred_element_type=F32)
    y = _norm_mod(x1, ng_ref[...], sh_ref[0], sc_ref[0]).astype(BF16)
    acc = jnp.zeros_like(x1)
    for c in range(wu_ref.shape[1] // fc):
        h = jnp.dot(y, wu_ref[:, c * fc:(c + 1) * fc], preferred_element_type=F32)
        h = jnp.square(jnp.maximum(h, 0.0)).astype(BF16)
        acc = acc + jnp.dot(h, wd_ref[c * fc:(c + 1) * fc, :], preferred_element_type=F32)
    x2 = x1 + g2_ref[0] * acc
    if final:
        x2 = _rms(x2, NORM_EPS) * fg_ref[...]
    o_ref[0] = x2


def _out_mlp(x, a, gate1, shift2, scale2, gate2, norm_g, w_out, w_up, w_down, final_g, *, final):
    bsz, s, d = x.shape
    da = a.shape[-1]
    f = w_up.shape[1]
    tm = _tile(s, 512)
    kern = functools.partial(_out_mlp_kernel, fc=_tile(f, 1024), final=final)
    vec = pl.BlockSpec((1, 1, d), lambda b, i: (b, 0, 0))
    return pl.pallas_call(
        kern,
        grid=(bsz, s // tm),
        in_specs=[
            pl.BlockSpec((1, tm, d), lambda b, i: (b, i, 0)),
            pl.BlockSpec((1, tm, da), lambda b, i: (b, i, 0)),
            vec, vec, vec, vec,
            _resident((1, d)),
            _resident((da, d)),
            _resident((d, f)),
            _resident((f, d)),
            _resident((1, d)),
        ],
        out_specs=pl.BlockSpec((1, tm, d), lambda b, i: (b, i, 0)),
        out_shape=jax.ShapeDtypeStruct((bsz, s, d), F32),
        compiler_params=_params(("arbitrary", "arbitrary")),
        name="out_mlp",
    )(x, a, gate1, shift2, scale2, gate2, norm_g.reshape(1, d), w_out, w_up, w_down, final_g.reshape(1, d))


def _lambda_init(layer_number):
    return 0.8 - 0.6 * math.exp(-0.3 * (layer_number - 1))


def kernel(x, c, positions, ada_w, ada_b, norm_mix_g, norm_mlp_g, mlp_w_up, mlp_w_down, mlstm_w_in, mlstm_b_gates, mlstm_head_g, mlstm_w_out, kv_norm_g, kv_ada_w, kv_ada_b, kv_w, diff_w_q, diff_lambda, diff_subln_g, diff_w_out, final_norm_g):
    bsz, s, d = x.shape
    depth = ada_w.shape[0]
    n_a = mlstm_w_in.shape[0]
    a_heads = mlstm_b_gates.shape[-1]
    a_v_cols = mlstm_w_out.shape[1]
    a_q_cols = (mlstm_w_in.shape[-1] - 2 * a_v_cols - 2 * a_heads) // 2
    a_dk, a_dv = a_q_cols // a_heads, a_v_cols // a_heads
    b_comp = diff_lambda.shape[-1]
    b_heads = diff_w_q.shape[-1] // (2 * b_comp)
    rope_dim = b_comp // 4

    mods = _mods(c, ada_w, ada_b).reshape(depth, bsz, 1, 6 * d)
    kv_mods = _mods(c, kv_ada_w[None], kv_ada_b[None]).reshape(bsz, 1, 2 * d)
    cos_t, sin_t = _rope_tables(positions, b_comp, rope_dim)

    k = vt = None
    for layer in range(depth):
        m = mods[layer]
        sh1, sc1, g1, sh2, sc2, g2 = [m[:, :, j * d:(j + 1) * d] for j in range(6)]
        if layer < n_a:
            w_in = mlstm_w_in[layer]
            n_main = 2 * a_q_cols + 2 * a_v_cols
            w_voq = jnp.concatenate([w_in[:, 2 * a_q_cols:n_main], w_in[:, :a_q_cols]], axis=1).astype(BF16)
            w_kt = w_in[:, a_q_cols:2 * a_q_cols].T.astype(BF16)
            w_gates_t = w_in[:, n_main:].T.astype(BF16)
            b_g = mlstm_b_gates[layer].reshape(2 * a_heads, 1)
            chunk = _tile(s, MLSTM_CHUNK)
            voq, kt, gates = _mlstm_proj(x, norm_mix_g[layer], sh1, sc1, w_voq, w_kt, w_gates_t, b_g,
                                         v_cols=a_v_cols, k_scale=a_dk ** -0.5, heads=a_heads, chunk=chunk)
            mix = _mlstm(voq, kt, gates, mlstm_head_g[layer], heads=a_heads, dk=a_dk, dv=a_dv, chunk=chunk)
            w_o = mlstm_w_out[layer]
        else:
            j = layer - n_a
            (q,) = _rope_proj(x, norm_mix_g[layer], sh1, sc1, diff_w_q[j].astype(BF16), cos_t, sin_t,
                              out_scale=b_comp ** -0.5 * LOG2E, half=rope_dim // 2)
            mix = _attention(q, k, vt, diff_lambda[j], diff_subln_g[j], heads=b_heads, lam_init=_lambda_init(layer + 1))
            w_o = diff_w_out[j]
        x = _out_mlp(x, mix, g1, sh2, sc2, g2, norm_mlp_g[layer], w_o.astype(BF16), mlp_w_up[layer].astype(BF16),
                     mlp_w_down[layer].astype(BF16), final_norm_g, final=(layer == depth - 1))
        if layer == n_a - 1:
            n_k = b_heads * 2 * b_comp
            k, vt = _rope_proj(x, kv_norm_g, kv_mods[:, :, :d], kv_mods[:, :, d:], kv_w[:, :n_k].astype(BF16), cos_t, sin_t,
                               kv_w[:, n_k:].T.astype(BF16), out_scale=1.0, half=rope_dim // 2)
    return x
```

```python
import functools
import math

import jax
import jax.numpy as jnp
from jax import lax
from jax.experimental import pallas as pl
from jax.experimental.pallas import tpu as pltpu

F32 = jnp.float32
BF16 = jnp.bfloat16

NORM_EPS = 1e-6
SUBLN_EPS = 1e-5
ROPE_THETA = 500000.0
ATTN_AHEAD = 2
MLSTM_CHUNK = 256

V7X_VMEM_LIMIT_BYTES = 52 * 1024 * 1024
LANES = 128
BF16_SUBLANES = 16
LOG2E = math.log2(math.e)

_NT = (((1,), (1,)), ((), ()))
_TN = (((0,), (0,)), ((), ()))


def _tile(n, pref):
    t = min(n, pref)
    assert n % t == 0, (n, t)
    return t


def _params(sem):
    return pltpu.CompilerParams(dimension_semantics=sem, vmem_limit_bytes=V7X_VMEM_LIMIT_BYTES)


def _resident(shape):
    nd = len(shape)
    return pl.BlockSpec(shape, lambda *_: (0,) * nd, pipeline_mode=pl.Buffered(1))


def _rms(x, eps):
    return x * lax.rsqrt(jnp.mean(x * x, axis=-1, keepdims=True) + eps)


def _norm_mod(x, g, shift, scale):
    return (_rms(x, NORM_EPS) * g) * (1.0 + scale) + shift


def _mods_kernel(c_ref, w_ref, b_ref, o_ref):
    c = c_ref[...]
    act = (c * jax.nn.sigmoid(c)).astype(BF16)
    o_ref[0] = jnp.dot(act, w_ref[0].astype(BF16), preferred_element_type=F32) + b_ref[0]


def _mods(c, w, b):
    nl, d, n = w.shape
    bsz = c.shape[0]
    tn = _tile(n, 1024)
    return pl.pallas_call(
        _mods_kernel,
        grid=(nl, n // tn),
        in_specs=[
            pl.BlockSpec((bsz, d), lambda l, j: (0, 0)),
            pl.BlockSpec((1, d, tn), lambda l, j: (l, 0, j)),
            pl.BlockSpec((1, 1, tn), lambda l, j: (l, 0, j)),
        ],
        out_specs=pl.BlockSpec((1, bsz, tn), lambda l, j: (l, 0, j)),
        out_shape=jax.ShapeDtypeStruct((nl, bsz, n), F32),
        compiler_params=_params(("arbitrary", "arbitrary")),
        name="mods",
    )(c, w, b.reshape(nl, 1, n))


def _rope_table_kernel(pos_ref, cos_ref, sin_ref, *, comp_dim, rope_dim):
    half = rope_dim // 2
    pos = pos_ref[...].astype(F32)
    lane = lax.broadcasted_iota(jnp.int32, (1, LANES), 1) % comp_dim
    fidx = (lane % half).astype(F32)
    inv_freq = jnp.exp(fidx * (-2.0 * math.log(ROPE_THETA) / rope_dim))
    ang = pos * inv_freq
    rot = lane < rope_dim
    cos_ref[...] = jnp.where(rot, jnp.cos(ang), 1.0)
    sin_ref[...] = jnp.where(rot, jnp.where(lane < half, -jnp.sin(ang), jnp.sin(ang)), 0.0)


def _rope_tables(positions, comp_dim, rope_dim):
    t = positions.size
    tm = _tile(t, 2048)
    return pl.pallas_call(
        functools.partial(_rope_table_kernel, comp_dim=comp_dim, rope_dim=rope_dim),
        grid=(t // tm,),
        in_specs=[pl.BlockSpec((tm, 1), lambda i: (i, 0))],
        out_specs=[pl.BlockSpec((tm, LANES), lambda i: (i, 0))] * 2,
        out_shape=[jax.ShapeDtypeStruct((t, LANES), F32)] * 2,
        compiler_params=_params(("arbitrary",)),
        name="rope_tables",
    )(positions.reshape(t, 1))


def _apply_rope(r, cos_t, sin_t, half):
    lane = lax.broadcasted_iota(jnp.int32, cos_t.shape, 1)
    first = (lane % (LANES // 2)) < half
    outs = []
    for j in range(r.shape[1] // LANES):
        blk = r[:, j * LANES:(j + 1) * LANES]
        partner = jnp.where(first, pltpu.roll(blk, LANES - half, axis=1), pltpu.roll(blk, half, axis=1))
        outs.append(blk * cos_t + partner * sin_t)
    return outs


def _rope_proj_kernel(x_ref, g_ref, sh_ref, sc_ref, w_ref, cos_ref, sin_ref, *rest, half, out_scale, nc, with_vt):
    y = _norm_mod(x_ref[0], g_ref[...], sh_ref[0], sc_ref[0]).astype(BF16)
    cos_t = cos_ref[...]
    sin_t = sin_ref[...]
    o_ref = rest[-2] if with_vt else rest[-1]
    for c in range(o_ref.shape[-1] // nc):
        r = jnp.dot(y, w_ref[:, c * nc:(c + 1) * nc], preferred_element_type=F32)
        for j, blk in enumerate(_apply_rope(r, cos_t, sin_t, half)):
            if out_scale != 1.0:
                blk = blk * out_scale
            o_ref[0, :, c * nc + j * LANES:c * nc + (j + 1) * LANES] = blk.astype(o_ref.dtype)
    if with_vt:
        wvt_ref, vt_ref = rest[0], rest[-1]
        for c in range(vt_ref.shape[1] // nc):
            r = lax.dot_general(wvt_ref[c * nc:(c + 1) * nc, :], y, _NT, preferred_element_type=F32)
            vt_ref[0, c * nc:(c + 1) * nc, :] = r.astype(vt_ref.dtype)


def _rope_proj(x, g, shift, scale, w, cos_t, sin_t, w_vt=None, *, out_scale, half):
    bsz, s, d = x.shape
    n = w.shape[1]
    tm = _tile(s, 512)
    nt = s // tm
    with_vt = w_vt is not None
    kern = functools.partial(_rope_proj_kernel, half=half, out_scale=out_scale, nc=_tile(n, 512), with_vt=with_vt)
    vec = pl.BlockSpec((1, 1, d), lambda b, i: (b, 0, 0))
    tab = pl.BlockSpec((tm, LANES), lambda b, i: (b * nt + i, 0))
    in_specs = [pl.BlockSpec((1, tm, d), lambda b, i: (b, i, 0)), _resident((1, d)), vec, vec, _resident((d, n)), tab, tab]
    out_specs = [pl.BlockSpec((1, tm, n), lambda b, i: (b, i, 0))]
    out_shape = [jax.ShapeDtypeStruct((bsz, s, n), BF16)]
    args = [x, g.reshape(1, d), shift, scale, w, cos_t, sin_t]
    if with_vt:
        nv = w_vt.shape[0]
        in_specs.append(_resident((nv, d)))
        out_specs.append(pl.BlockSpec((1, nv, tm), lambda b, i: (b, 0, i)))
        out_shape.append(jax.ShapeDtypeStruct((bsz, nv, s), BF16))
        args.append(w_vt)
    return pl.pallas_call(
        kern,
        grid=(bsz, nt),
        in_specs=in_specs,
        out_specs=out_specs,
        out_shape=out_shape,
        compiler_params=_params(("arbitrary", "arbitrary")),
        name="rope_proj",
    )(*args)


def _mlstm_proj_kernel(x_ref, g_ref, sh_ref, sc_ref, w_ref, wkt_ref, wg_ref, bg_ref, voq_ref, kt_ref, gates_ref, *, v_cols, k_scale, heads, nc, chunk):
    y = _norm_mod(x_ref[0], g_ref[...], sh_ref[0], sc_ref[0]).astype(BF16)
    for c in range(w_ref.shape[1] // nc):
        col = c * nc
        r = jnp.dot(y, w_ref[:, col:col + nc], preferred_element_type=F32)
        if v_cols <= col < 2 * v_cols:
            r = jax.nn.sigmoid(r)
        voq_ref[0, :, col:col + nc] = r.astype(BF16)
    kt = lax.dot_general(wkt_ref[...], y, _NT, preferred_element_type=F32) * k_scale
    kt_ref[0] = kt.astype(BF16)
    gt = lax.dot_general(wg_ref[...], y, _NT, preferred_element_type=F32) + bg_ref[...]
    row = lax.broadcasted_iota(jnp.int32, gt.shape, 0)
    logf = jnp.minimum(gt, 0.0) - jnp.log1p(jnp.exp(-jnp.abs(gt)))
    gt = jnp.where(row < heads, gt, logf)
    for c in range(gt.shape[1] // chunk):
        gates_ref[0, c] = gt[:, c * chunk:(c + 1) * chunk]


def _mlstm_proj(x, g, shift, scale, w_voq, w_kt, w_gates_t, b_gates, *, v_cols, k_scale, heads, chunk):
    bsz, s, d = x.shape
    n = w_voq.shape[1]
    q_cols = w_kt.shape[0]
    tm = _tile(s, 512)
    kern = functools.partial(_mlstm_proj_kernel, v_cols=v_cols, k_scale=k_scale, heads=heads,
                             nc=_tile(q_cols, 512), chunk=chunk)
    vec = pl.BlockSpec((1, 1, d), lambda b, i: (b, 0, 0))
    return pl.pallas_call(
        kern,
        grid=(bsz, s // tm),
        in_specs=[
            pl.BlockSpec((1, tm, d), lambda b, i: (b, i, 0)),
            _resident((1, d)), vec, vec,
            _resident((d, n)),
            _resident((q_cols, d)),
            _resident((2 * heads, d)),
            _resident((2 * heads, 1)),
        ],
        out_specs=[
            pl.BlockSpec((1, tm, n), lambda b, i: (b, i, 0)),
            pl.BlockSpec((1, q_cols, tm), lambda b, i: (b, 0, i)),
            pl.BlockSpec((1, tm // chunk, 2 * heads, chunk), lambda b, i: (b, i, 0, 0)),
        ],
        out_shape=[
            jax.ShapeDtypeStruct((bsz, s, n), BF16),
            jax.ShapeDtypeStruct((bsz, q_cols, s), BF16),
            jax.ShapeDtypeStruct((bsz, s // chunk, 2 * heads, chunk), F32),
        ],
        compiler_params=_params(("arbitrary", "arbitrary")),
        name="mlstm_proj",
    )(x, g.reshape(1, d), shift, scale, w_voq, w_kt, w_gates_t, b_gates)


def _prefix_max(x):
    n = x.shape[1]
    lane = lax.broadcasted_iota(jnp.int32, x.shape, 1)
    shift = 1
    while shift < n:
        x = jnp.maximum(x, jnp.where(lane >= shift, pltpu.roll(x, shift, axis=1), -jnp.inf))
        shift *= 2
    return x


def _split3(x):
    hi = x.astype(BF16)
    r1 = x - hi.astype(F32)
    mid = r1.astype(BF16)
    lo = (r1 - mid.astype(F32)).astype(BF16)
    return hi, mid, lo


def _mlstm_kernel(v_ref, og_ref, q_ref, kt_ref, gates_ref, hg_ref, out_ref, c_sc, n_sc, m_sc, *, heads, dk, dv):
    H = heads
    L = q_ref.shape[1]

    @pl.when(pl.program_id(1) == 0)
    def _():
        c_sc[...] = jnp.zeros_like(c_sc)
        n_sc[...] = jnp.zeros_like(n_sc)
        m_sc[...] = jnp.zeros_like(m_sc)

    row = lax.broadcasted_iota(jnp.int32, (L, L), 0)
    col = lax.broadcasted_iota(jnp.int32, (L, L), 1)
    causal = col <= row
    upper = (row <= col).astype(F32)

    gr = gates_ref[0, 0]
    g = jnp.dot(gr, upper, precision=lax.Precision.HIGHEST, preferred_element_type=F32)[H:]
    b = gr[:H] - g
    cm = _prefix_max(b)
    rows = jnp.concatenate([cm, g], axis=0)
    stacked = jnp.concatenate(list(_split3(rows)) + [jnp.zeros_like(rows, dtype=BF16)], axis=0)
    sel_r = lax.broadcasted_iota(jnp.int32, (8 * H, 2 * H * LANES), 0) % (2 * H)
    sel_c = lax.broadcasted_iota(jnp.int32, (8 * H, 2 * H * LANES), 1) // LANES
    sel = (sel_r == sel_c).astype(BF16)
    cols = lax.dot_general(stacked, sel, _TN, preferred_element_type=F32)
    ones = jnp.ones((L, LANES), BF16)
    rep_l = L // LANES
    rep_v = dv // LANES

    for h in range(H):
        cm_c = cols[:, h * LANES:(h + 1) * LANES]
        g_c = cols[:, (H + h) * LANES:(H + h + 1) * LANES]
        b_row = b[h:h + 1]
        g_tot = g[h:h + 1, L - 1:L]
        m_old = m_sc[h]
        m_c = jnp.maximum(cm_c, m_old)
        m_end = jnp.maximum(m_old, jnp.max(b_row, axis=-1, keepdims=True))
        qh = q_ref[0, :, h * dk:(h + 1) * dk]
        kth = kt_ref[0, h * dk:(h + 1) * dk, :]
        vh = v_ref[0, :, h * dv:(h + 1) * dv]
        oh = og_ref[0, :, h * dv:(h + 1) * dv]
        c_old = c_sc[h]
        n_old = n_sc[h]

        s = jnp.dot(qh, kth, preferred_element_type=F32)
        w = jnp.exp(jnp.where(causal, b_row - jnp.tile(m_c, (1, rep_l)), -jnp.inf))
        p = (s * w).astype(BF16)
        w_inter = jnp.exp(m_old - m_c)
        num = jnp.dot(p, vh, preferred_element_type=F32)
        num = num + jnp.tile(w_inter, (1, rep_v)) * jnp.dot(qh, c_old.astype(BF16), preferred_element_type=F32)
        den = jnp.dot(p, ones, preferred_element_type=F32)
        den = den + w_inter * jnp.dot(qh, n_old.astype(BF16), preferred_element_type=F32)
        inv = 1.0 / jnp.maximum(jnp.abs(den), jnp.exp(-(g_c + m_c)))
        hh = num * jnp.tile(inv, (1, rep_v))
        hh = _rms(hh, NORM_EPS) * hg_ref[:, h * dv:(h + 1) * dv]
        out_ref[0, :, h * dv:(h + 1) * dv] = (hh * oh.astype(F32)).astype(out_ref.dtype)

        w_old = jnp.exp(m_old - m_end)
        kw = (kth.astype(F32) * jnp.exp(b_row - m_end)).astype(BF16)
        c_sc[h] = w_old * c_old + jnp.dot(kw, vh, preferred_element_type=F32)
        n_sc[h] = w_old * n_old + jnp.dot(kw, ones, preferred_element_type=F32)
        m_sc[h] = g_tot + m_end


def _mlstm(voq, kt, gates, head_g, *, heads, dk, dv, chunk):
    bsz, s, _ = voq.shape
    qc, vc = heads * dk, heads * dv
    kern = functools.partial(_mlstm_kernel, heads=heads, dk=dk, dv=dv)
    return pl.pallas_call(
        kern,
        grid=(bsz, s // chunk),
        in_specs=[
            pl.BlockSpec((1, chunk, vc), lambda b, i: (b, i, 0)),
            pl.BlockSpec((1, chunk, vc), lambda b, i: (b, i, 1)),
            pl.BlockSpec((1, chunk, qc), lambda b, i: (b, i, (2 * vc) // qc)),
            pl.BlockSpec((1, qc, chunk), lambda b, i: (b, 0, i)),
            pl.BlockSpec((1, 1, 2 * heads, chunk), lambda b, i: (b, i, 0, 0)),
            _resident((1, vc)),
        ],
        out_specs=pl.BlockSpec((1, chunk, vc), lambda b, i: (b, i, 0)),
        out_shape=jax.ShapeDtypeStruct((bsz, s, vc), BF16),
        scratch_shapes=[
            pltpu.VMEM((heads, dk, dv), F32),
            pltpu.VMEM((heads, dk, LANES), F32),
            pltpu.VMEM((heads, 1, 1), F32),
        ],
        compiler_params=_params(("arbitrary", "arbitrary")),
        name="mlstm",
    )(voq, voq, voq, kt, gates, head_g.reshape(1, vc))


def _attn_kernel(q_ref, k_ref, vt_ref, lam_ref, g_ref, bias_ref, o_ref, m_sc, acc_sc, s_sc, *, tq, tk, hd, nh, ahead, lam_init):
    lp = lam_ref[...]
    lam = (jnp.exp(jnp.sum(lp[0:1] * lp[1:2], axis=-1, keepdims=True))
           - jnp.exp(jnp.sum(lp[2:3] * lp[3:4], axis=-1, keepdims=True)) + lam_init)
    g_col = g_ref[...] * (1.0 - lam_init)
    comp_dim = hd // 2
    assert tq == tk and ahead <= nh

    def q_tile(qi, carry):
        q0 = pl.multiple_of(qi * tq, tq)
        qqs = []
        for h in range(nh):
            q = q_ref[0, pl.ds(q0, tq), h * hd:(h + 1) * hd]
            lane = lax.broadcasted_iota(jnp.int32, q.shape, 1)
            zero = jnp.zeros_like(q)
            qqs.append(jnp.concatenate([jnp.where(lane < comp_dim, q, zero), jnp.where(lane >= comp_dim, q, zero)], axis=0))
        m_sc[...] = jnp.full_like(m_sc, -jnp.inf)
        acc_sc[...] = jnp.zeros_like(acc_sc)
        ones = jnp.ones((acc_sc.shape[1] - hd, tk), BF16)

        def scores(k0, h):
            kk = k_ref[0, pl.ds(k0, tk), h * hd:(h + 1) * hd]
            return lax.dot_general(kk, qqs[h], _NT, preferred_element_type=F32)

        def step(j, c):
            k0 = pl.multiple_of(j * tk, tk)
            k_next = pl.multiple_of(jnp.minimum(k0 + tk, k_ref.shape[1] - tk), tk)
            diag = (j == qi).astype(jnp.int32)
            pending = [s_sc[i] for i in range(ahead)]
            for h in range(nh):
                s = jnp.minimum(pending.pop(0), bias_ref[diag])
                if h + ahead < nh:
                    pending.append(scores(k0, h + ahead))
                else:
                    s_sc[h + ahead - nh] = scores(k_next, h + ahead - nh)
                vt = jnp.concatenate([vt_ref[0, h * hd:(h + 1) * hd, pl.ds(k0, tk)], ones], axis=0)
                m_old = m_sc[h]
                m_new = jnp.maximum(m_old, jnp.max(s, axis=0, keepdims=True))
                alpha = jnp.exp2(m_old - m_new)
                p = jnp.exp2(s - m_new).astype(BF16)
                acc_sc[h] = alpha * acc_sc[h] + jnp.dot(vt, p, preferred_element_type=F32)
                m_sc[h] = m_new
            return c

        for i in range(ahead):
            s_sc[i] = scores(0, i)
        lax.fori_loop(0, qi + 1, step, 0)

        for h in range(nh):
            acc = acc_sc[h]
            acc = acc[:hd] / acc[hd:hd + 1]
            o = acc[:, :tq] - lam * acc[:, tq:]
            o = o * lax.rsqrt(jnp.mean(o * o, axis=0, keepdims=True) + SUBLN_EPS) * g_col
            o_ref[0, pl.ds(q0, tq), h * hd:(h + 1) * hd] = o.T.astype(o_ref.dtype)
        return carry

    lax.fori_loop(0, q_ref.shape[1] // tq, q_tile, 0)


def _attention(q, k, vt, lam_params, subln_g, *, heads, lam_init):
    bsz, s, d = q.shape
    hd = d // heads
    tq = _tile(s, 256)
    tk = tq
    nh = _tile(heads, 8)
    kern = functools.partial(_attn_kernel, tq=tq, tk=tk, hd=hd, nh=nh, ahead=ATTN_AHEAD, lam_init=lam_init)
    kpos = lax.broadcasted_iota(jnp.int32, (tk, 2 * tq), 0)
    qpos = lax.broadcasted_iota(jnp.int32, (tk, 2 * tq), 1) % tq
    bias = jnp.stack([jnp.full((tk, 2 * tq), jnp.inf, F32), jnp.where(kpos <= qpos, jnp.inf, -jnp.inf)])
    return pl.pallas_call(
        kern,
        grid=(bsz, heads // nh),
        in_specs=[
            pl.BlockSpec((1, s, nh * hd), lambda b, h: (b, 0, h)),
            pl.BlockSpec((1, s, nh * hd), lambda b, h: (b, 0, h)),
            pl.BlockSpec((1, nh * hd, s), lambda b, h: (b, h, 0)),
            _resident(lam_params.shape),
            _resident((hd, 1)),
            _resident(bias.shape),
        ],
        out_specs=pl.BlockSpec((1, s, nh * hd), lambda b, h: (b, 0, h)),
        out_shape=jax.ShapeDtypeStruct((bsz, s, d), BF16),
        scratch_shapes=[
            pltpu.VMEM((nh, 1, 2 * tq), F32),
            pltpu.VMEM((nh, hd + BF16_SUBLANES, 2 * tq), F32),
            pltpu.VMEM((ATTN_AHEAD, tk, 2 * tq), F32),
        ],
        compiler_params=_params(("arbitrary", "arbitrary")),
        name="diff_attn",
    )(q, k, vt, lam_params, subln_g.reshape(hd, 1), bias)


def _out_mlp_kernel(x_ref, a_ref, g1_ref, sh_ref, sc_ref, g2_ref, ng_ref, wo_ref, wu_ref, wd_ref, fg_ref, o_ref, *, fc, final):
    x1 = x_ref[0] + g1_ref[0] * jnp.dot(a_ref[0], wo_ref[...], preferred_element_type=F32)
    y = _norm_mod(x1, ng_ref[...], sh_ref[0], sc_ref[0]).astype(BF16)
    acc = jnp.zeros_like(x1)
    for c in range(wu_ref.shape[1] // fc):
        h = jnp.dot(y, wu_ref[:, c * fc:(c + 1) * fc], preferred_element_type=F32)
        h = jnp.square(jnp.maximum(h, 0.0)).astype(BF16)
        acc = acc + jnp.dot(h, wd_ref[c * fc:(c + 1) * fc, :], preferred_element_type=F32)
    x2 = x1 + g2_ref[0] * acc
    if final:
        x2 = _rms(x2, NORM_EPS) * fg_ref[...]
    o_ref[0] = x2


def _out_mlp(x, a, gate1, shift2, scale2, gate2, norm_g, w_out, w_up, w_down, final_g, *, final):
    bsz, s, d = x.shape
    da = a.shape[-1]
    f = w_up.shape[1]
    tm = _tile(s, 512)
    kern = functools.partial(_out_mlp_kernel, fc=_tile(f, 1024), final=final)
    vec = pl.BlockSpec((1, 1, d), lambda b, i: (b, 0, 0))
    return pl.pallas_call(
        kern,
        grid=(bsz, s // tm),
        in_specs=[
            pl.BlockSpec((1, tm, d), lambda b, i: (b, i, 0)),
            pl.BlockSpec((1, tm, da), lambda b, i: (b, i, 0)),
            vec, vec, vec, vec,
            _resident((1, d)),
            _resident((da, d)),
            _resident((d, f)),
            _resident((f, d)),
            _resident((1, d)),
        ],
        out_specs=pl.BlockSpec((1, tm, d), lambda b, i: (b, i, 0)),
        out_shape=jax.ShapeDtypeStruct((bsz, s, d), F32),
        compiler_params=_params(("arbitrary", "arbitrary")),
        name="out_mlp",
    )(x, a, gate1, shift2, scale2, gate2, norm_g.reshape(1, d), w_out, w_up, w_down, final_g.reshape(1, d))


def _lambda_init(layer_number):
    return 0.8 - 0.6 * math.exp(-0.3 * (layer_number - 1))


def kernel(x, c, positions, ada_w, ada_b, norm_mix_g, norm_mlp_g, mlp_w_up, mlp_w_down, mlstm_w_in, mlstm_b_gates, mlstm_head_g, mlstm_w_out, kv_norm_g, kv_ada_w, kv_ada_b, kv_w, diff_w_q, diff_lambda, diff_subln_g, diff_w_out, final_norm_g):
    bsz, s, d = x.shape
    depth = ada_w.shape[0]
    n_a = mlstm_w_in.shape[0]
    a_heads = mlstm_b_gates.shape[-1]
    a_v_cols = mlstm_w_out.shape[1]
    a_q_cols = (mlstm_w_in.shape[-1] - 2 * a_v_cols - 2 * a_heads) // 2
    a_dk, a_dv = a_q_cols // a_heads, a_v_cols // a_heads
    b_comp = diff_lambda.shape[-1]
    b_heads = diff_w_q.shape[-1] // (2 * b_comp)
    rope_dim = b_comp // 4

    mods = _mods(c, ada_w, ada_b).reshape(depth, bsz, 1, 6 * d)
    kv_mods = _mods(c, kv_ada_w[None], kv_ada_b[None]).reshape(bsz, 1, 2 * d)
    cos_t, sin_t = _rope_tables(positions, b_comp, rope_dim)

    k = vt = None
    for layer in range(depth):
        m = mods[layer]
        sh1, sc1, g1, sh2, sc2, g2 = [m[:, :, j * d:(j + 1) * d] for j in range(6)]
        if layer < n_a:
            w_in = mlstm_w_in[layer]
            n_main = 2 * a_q_cols + 2 * a_v_cols
            w_voq = jnp.concatenate([w_in[:, 2 * a_q_cols:n_main], w_in[:, :a_q_cols]], axis=1).astype(BF16)
            w_kt = w_in[:, a_q_cols:2 * a_q_cols].T.astype(BF16)
            w_gates_t = w_in[:, n_main:].T.astype(BF16)
            b_g = mlstm_b_gates[layer].reshape(2 * a_heads, 1)
            chunk = _tile(s, MLSTM_CHUNK)
            voq, kt, gates = _mlstm_proj(x, norm_mix_g[layer], sh1, sc1, w_voq, w_kt, w_gates_t, b_g,
                                         v_cols=a_v_cols, k_scale=a_dk ** -0.5, heads=a_heads, chunk=chunk)
            mix = _mlstm(voq, kt, gates, mlstm_head_g[layer], heads=a_heads, dk=a_dk, dv=a_dv, chunk=chunk)
            w_o = mlstm_w_out[layer]
        else:
            j = layer - n_a
            (q,) = _rope_proj(x, norm_mix_g[layer], sh1, sc1, diff_w_q[j].astype(BF16), cos_t, sin_t,
                              out_scale=b_comp ** -0.5 * LOG2E, half=rope_dim // 2)
            mix = _attention(q, k, vt, diff_lambda[j], diff_subln_g[j], heads=b_heads, lam_init=_lambda_init(layer + 1))
            w_o = diff_w_out[j]
        x = _out_mlp(x, mix, g1, sh2, sc2, g2, norm_mlp_g[layer], w_o.astype(BF16), mlp_w_up[layer].astype(BF16),
                     mlp_w_down[layer].astype(BF16), final_norm_g, final=(layer == depth - 1))
        if layer == n_a - 1:
            n_k = b_heads * 2 * b_comp
            k, vt = _rope_proj(x, kv_norm_g, kv_mods[:, :, :d], kv_mods[:, :, d:], kv_w[:, :n_k].astype(BF16), cos_t, sin_t,
                               kv_w[:, n_k:].T.astype(BF16), out_scale=1.0, half=rope_dim // 2)
    return x
```

```python
import functools
import math

import jax
import jax.numpy as jnp
from jax import lax
from jax.experimental import pallas as pl
from jax.experimental.pallas import tpu as pltpu

F32 = jnp.float32
BF16 = jnp.bfloat16

NORM_EPS = 1e-6
SUBLN_EPS = 1e-5
ROPE_THETA = 500000.0
ATTN_AHEAD = 2
MLSTM_CHUNK = 256

V7X_VMEM_LIMIT_BYTES = 52 * 1024 * 1024
LANES = 128
BF16_SUBLANES = 16
LOG2E = math.log2(math.e)

_NT = (((1,), (1,)), ((), ()))
_TN = (((0,), (0,)), ((), ()))


def _tile(n, pref):
    t = min(n, pref)
    assert n % t == 0, (n, t)
    return t


def _params(sem):
    return pltpu.CompilerParams(dimension_semantics=sem, vmem_limit_bytes=V7X_VMEM_LIMIT_BYTES)


def _resident(shape):
    nd = len(shape)
    return pl.BlockSpec(shape, lambda *_: (0,) * nd, pipeline_mode=pl.Buffered(1))


def _rms(x, eps):
    return x * lax.rsqrt(jnp.mean(x * x, axis=-1, keepdims=True) + eps)


def _norm_mod(x, g, shift, scale):
    return (_rms(x, NORM_EPS) * g) * (1.0 + scale) + shift


def _mods_kernel(c_ref, w_ref, b_ref, o_ref):
    c = c_ref[...]
    act = (c * jax.nn.sigmoid(c)).astype(BF16)
    o_ref[0] = jnp.dot(act, w_ref[0].astype(BF16), preferred_element_type=F32) + b_ref[0]


def _mods(c, w, b):
    nl, d, n = w.shape
    bsz = c.shape[0]
    tn = _tile(n, 1024)
    return pl.pallas_call(
        _mods_kernel,
        grid=(nl, n // tn),
        in_specs=[
            pl.BlockSpec((bsz, d), lambda l, j: (0, 0)),
            pl.BlockSpec((1, d, tn), lambda l, j: (l, 0, j)),
            pl.BlockSpec((1, 1, tn), lambda l, j: (l, 0, j)),
        ],
        out_specs=pl.BlockSpec((1, bsz, tn), lambda l, j: (l, 0, j)),
        out_shape=jax.ShapeDtypeStruct((nl, bsz, n), F32),
        compiler_params=_params(("arbitrary", "arbitrary")),
        name="mods",
    )(c, w, b.reshape(nl, 1, n))


def _rope_tables(pos_row, half):
    comp_dim = LANES // 2
    tm = pos_row.shape[1]
    fidx = lax.broadcasted_iota(jnp.int32, (half, 1), 0).astype(F32)
    inv_freq = jnp.exp(fidx * (-math.log(ROPE_THETA) / half))
    ang = inv_freq * pos_row.astype(F32)
    rows = 4 * half
    r_idx = lax.broadcasted_iota(jnp.int32, (rows, LANES), 0)
    lane = lax.broadcasted_iota(jnp.int32, (rows, LANES), 1) % comp_dim
    hit = (r_idx < 3 * half) & (r_idx % half == lane % half)
    sel_cos = jnp.where(hit & (lane < 2 * half), 1.0, 0.0).astype(BF16)
    sel_sin = jnp.where(hit & (lane < half), -1.0, jnp.where(hit & (lane < 2 * half), 1.0, 0.0)).astype(BF16)
    lane1 = lax.broadcasted_iota(jnp.int32, (1, LANES), 1) % comp_dim
    ones_elsewhere = jnp.where(lane1 < 2 * half, 0.0, 1.0)

    def expand(v, sel):
        stacked = jnp.concatenate(list(_split3(v)) + [jnp.zeros((half, tm), BF16)], axis=0)
        return lax.dot_general(stacked, sel, _TN, preferred_element_type=F32)

    return expand(jnp.cos(ang), sel_cos) + ones_elsewhere, expand(jnp.sin(ang), sel_sin)


def _apply_rope(r, cos_t, sin_t, half):
    lane = lax.broadcasted_iota(jnp.int32, cos_t.shape, 1)
    pos = lane % (LANES // 2)
    src = jnp.where(pos < half, lane + half, jnp.where(pos < 2 * half, lane - half, lane))
    outs = []
    for j in range(r.shape[1] // LANES):
        blk = r[:, j * LANES:(j + 1) * LANES]
        partner = jnp.take_along_axis(blk, src, axis=1)
        outs.append(blk * cos_t + partner * sin_t)
    return outs


def _rope_proj_kernel(x_ref, g_ref, sh_ref, sc_ref, w_ref, pos_ref, *rest, half, out_scale, nc, with_vt):
    y = _norm_mod(x_ref[0], g_ref[...], sh_ref[0], sc_ref[0]).astype(BF16)
    cos_t, sin_t = _rope_tables(pos_ref[0, 0], half)
    o_ref = rest[-2] if with_vt else rest[-1]
    for c in range(o_ref.shape[-1] // nc):
        r = jnp.dot(y, w_ref[:, c * nc:(c + 1) * nc], preferred_element_type=F32)
        for j, blk in enumerate(_apply_rope(r, cos_t, sin_t, half)):
            if out_scale != 1.0:
                blk = blk * out_scale
            o_ref[0, :, c * nc + j * LANES:c * nc + (j + 1) * LANES] = blk.astype(o_ref.dtype)
    if with_vt:
        wvt_ref, vt_ref = rest[0], rest[-1]
        for c in range(vt_ref.shape[1] // nc):
            r = lax.dot_general(wvt_ref[c * nc:(c + 1) * nc, :], y, _NT, preferred_element_type=F32)
            vt_ref[0, c * nc:(c + 1) * nc, :] = r.astype(vt_ref.dtype)


def _rope_proj(x, g, shift, scale, w, positions, w_vt=None, *, out_scale, half):
    bsz, s, d = x.shape
    n = w.shape[1]
    tm = _tile(s, 512)
    nt = s // tm
    with_vt = w_vt is not None
    kern = functools.partial(_rope_proj_kernel, half=half, out_scale=out_scale, nc=_tile(n, 512), with_vt=with_vt)
    vec = pl.BlockSpec((1, 1, d), lambda b, i: (b, 0, 0))
    pos = pl.BlockSpec((1, 1, 1, tm), lambda b, i: (b, i, 0, 0))
    in_specs = [pl.BlockSpec((1, tm, d), lambda b, i: (b, i, 0)), _resident((1, d)), vec, vec, _resident((d, n)), pos]
    out_specs = [pl.BlockSpec((1, tm, n), lambda b, i: (b, i, 0))]
    out_shape = [jax.ShapeDtypeStruct((bsz, s, n), BF16)]
    args = [x, g.reshape(1, d), shift, scale, w, positions.reshape(bsz, nt, 1, tm)]
    if with_vt:
        nv = w_vt.shape[0]
        in_specs.append(_resident((nv, d)))
        out_specs.append(pl.BlockSpec((1, nv, tm), lambda b, i: (b, 0, i)))
        out_shape.append(jax.ShapeDtypeStruct((bsz, nv, s), BF16))
        args.append(w_vt)
    return pl.pallas_call(
        kern,
        grid=(bsz, nt),
        in_specs=in_specs,
        out_specs=out_specs,
        out_shape=out_shape,
        compiler_params=_params(("arbitrary", "arbitrary")),
        name="rope_proj",
    )(*args)


def _mlstm_proj_kernel(x_ref, g_ref, sh_ref, sc_ref, w_ref, wkt_ref, wg_ref, bg_ref, hg_ref, voq_ref, kt_ref, gates_ref, *, v_cols, k_scale, heads, nc, chunk):
    y = _norm_mod(x_ref[0], g_ref[...], sh_ref[0], sc_ref[0]).astype(BF16)
    for c in range(w_ref.shape[1] // nc):
        col = c * nc
        r = jnp.dot(y, w_ref[:, col:col + nc], preferred_element_type=F32)
        if v_cols <= col < 2 * v_cols:
            r = jax.nn.sigmoid(r) * hg_ref[:, col - v_cols:col - v_cols + nc]
        voq_ref[0, :, col:col + nc] = r.astype(BF16)
    kt = lax.dot_general(wkt_ref[...], y, _NT, preferred_element_type=F32) * k_scale
    kt_ref[0] = kt.astype(BF16)
    gt = lax.dot_general(wg_ref[...], y, _NT, preferred_element_type=F32) + bg_ref[...]
    row = lax.broadcasted_iota(jnp.int32, gt.shape, 0)
    logf = jnp.minimum(gt, 0.0) - jnp.log1p(jnp.exp(-jnp.abs(gt)))
    gt = jnp.where(row < heads, gt, logf)
    for c in range(gt.shape[1] // chunk):
        gates_ref[0, c] = gt[:, c * chunk:(c + 1) * chunk]


def _mlstm_proj(x, g, shift, scale, w_voq, w_kt, w_gates_t, b_gates, head_g, *, v_cols, k_scale, heads, chunk):
    bsz, s, d = x.shape
    n = w_voq.shape[1]
    q_cols = w_kt.shape[0]
    tm = _tile(s, 512)
    kern = functools.partial(_mlstm_proj_kernel, v_cols=v_cols, k_scale=k_scale, heads=heads,
                             nc=_tile(q_cols, 512), chunk=chunk)
    vec = pl.BlockSpec((1, 1, d), lambda b, i: (b, 0, 0))
    return pl.pallas_call(
        kern,
        grid=(bsz, s // tm),
        in_specs=[
            pl.BlockSpec((1, tm, d), lambda b, i: (b, i, 0)),
            _resident((1, d)), vec, vec,
            _resident((d, n)),
            _resident((q_cols, d)),
            _resident((2 * heads, d)),
            _resident((2 * heads, 1)),
            _resident((1, v_cols)),
        ],
        out_specs=[
            pl.BlockSpec((1, tm, n), lambda b, i: (b, i, 0)),
            pl.BlockSpec((1, q_cols, tm), lambda b, i: (b, 0, i)),
            pl.BlockSpec((1, tm // chunk, 2 * heads, chunk), lambda b, i: (b, i, 0, 0)),
        ],
        out_shape=[
            jax.ShapeDtypeStruct((bsz, s, n), BF16),
            jax.ShapeDtypeStruct((bsz, q_cols, s), BF16),
            jax.ShapeDtypeStruct((bsz, s // chunk, 2 * heads, chunk), F32),
        ],
        compiler_params=_params(("arbitrary", "arbitrary")),
        name="mlstm_proj",
    )(x, g.reshape(1, d), shift, scale, w_voq, w_kt, w_gates_t, b_gates, head_g.reshape(1, v_cols))


def _prefix_max(x):
    n = x.shape[1]
    lane = lax.broadcasted_iota(jnp.int32, x.shape, 1)
    shift = 1
    while shift < n:
        x = jnp.maximum(x, jnp.where(lane >= shift, pltpu.roll(x, shift, axis=1), -jnp.inf))
        shift *= 2
    return x


def _split3(x):
    hi = x.astype(BF16)
    r1 = x - hi.astype(F32)
    mid = r1.astype(BF16)
    lo = (r1 - mid.astype(F32)).astype(BF16)
    return hi, mid, lo


def _mlstm_kernel(v_ref, og_ref, q_ref, kt_ref, gates_ref, out_ref, c_sc, n_sc, m_sc, *, heads, dk, dv):
    H = heads
    L = q_ref.shape[1]

    @pl.when(pl.program_id(1) == 0)
    def _():
        c_sc[...] = jnp.zeros_like(c_sc)
        n_sc[...] = jnp.zeros_like(n_sc)
        m_sc[...] = jnp.zeros_like(m_sc)

    row = lax.broadcasted_iota(jnp.int32, (L, L), 0)
    col = lax.broadcasted_iota(jnp.int32, (L, L), 1)
    causal = col <= row
    upper = (row <= col).astype(F32)

    ones = jnp.ones((L, LANES), BF16)
    ones_v = jnp.ones((dv, LANES), BF16)
    rep_l = L // LANES
    rep_v = dv // LANES
    rep_k = dk // LANES

    hs = range(H)
    m_old = [m_sc[h] for h in hs]
    c_old = [c_sc[h] for h in hs]
    n_old = [n_sc[h] for h in hs]
    qh = [q_ref[0, :, h * dk:(h + 1) * dk] for h in hs]
    kth = [kt_ref[0, h * dk:(h + 1) * dk, :] for h in hs]
    vh = [v_ref[0, :, h * dv:(h + 1) * dv] for h in hs]

    gr = gates_ref[0, 0]
    g = jnp.dot(gr, upper, precision=lax.Precision.HIGHEST, preferred_element_type=F32)[H:]
    s = [jnp.dot(qh[h], kth[h], preferred_element_type=F32) for h in hs]
    b = gr[:H] - g
    b_row = [b[h:h + 1] for h in hs]
    m_end = [jnp.maximum(m_old[h], jnp.max(b_row[h], axis=-1, keepdims=True)) for h in hs]
    c_new, n_new = [], []
    for h in hs:
        w_old = jnp.exp(m_old[h] - m_end[h])
        kw = (kth[h].astype(F32) * jnp.exp(b_row[h] - m_end[h])).astype(BF16)
        c_new.append(w_old * c_old[h] + jnp.dot(kw, vh[h], preferred_element_type=F32))
        n_new.append(w_old * n_old[h] + jnp.dot(kw, ones, preferred_element_type=F32))

    cm = _prefix_max(b)
    rows = jnp.concatenate([cm, g], axis=0)
    stacked = jnp.concatenate(list(_split3(rows)) + [jnp.zeros_like(rows, dtype=BF16)], axis=0)
    sel_r = lax.broadcasted_iota(jnp.int32, (8 * H, 2 * H * LANES), 0) % (2 * H)
    sel_c = lax.broadcasted_iota(jnp.int32, (8 * H, 2 * H * LANES), 1) // LANES
    sel = (sel_r == sel_c).astype(BF16)
    cols = lax.dot_general(stacked, sel, _TN, preferred_element_type=F32)
    m_c = [jnp.maximum(cols[:, h * LANES:(h + 1) * LANES], m_old[h]) for h in hs]

    num, den = [], []
    for h in hs:
        w = jnp.exp(jnp.where(causal, b_row[h] - jnp.tile(m_c[h], (1, rep_l)), -jnp.inf))
        p = (s[h] * w).astype(BF16)
        w_inter = jnp.exp(m_old[h] - m_c[h])
        qw = (qh[h].astype(F32) * jnp.tile(w_inter, (1, rep_k))).astype(BF16)
        lhs = jnp.concatenate([p, qw], axis=1)
        num.append(jnp.dot(lhs, jnp.concatenate([vh[h], c_old[h].astype(BF16)], axis=0), preferred_element_type=F32))
        den.append(jnp.dot(lhs, jnp.concatenate([ones, n_old[h].astype(BF16)], axis=0), preferred_element_type=F32))

    ssq = [jnp.dot((num[h] * num[h]).astype(BF16), ones_v, preferred_element_type=F32) for h in hs]
    for h in hs:
        g_c = cols[:, (H + h) * LANES:(H + h + 1) * LANES]
        inv = 1.0 / jnp.maximum(jnp.abs(den[h]), jnp.exp(-(g_c + m_c[h])))
        scale = inv * lax.rsqrt(inv * inv * ssq[h] * (1.0 / dv) + NORM_EPS)
        out = num[h] * jnp.tile(scale, (1, rep_v)) * og_ref[0, :, h * dv:(h + 1) * dv].astype(F32)
        out_ref[0, :, h * dv:(h + 1) * dv] = out.astype(out_ref.dtype)

    for h in hs:
        c_sc[h] = c_new[h]
        n_sc[h] = n_new[h]
        m_sc[h] = g[h:h + 1, L - 1:L] + m_end[h]


def _mlstm(voq, kt, gates, *, heads, dk, dv, chunk):
    bsz, s, _ = voq.shape
    qc, vc = heads * dk, heads * dv
    kern = functools.partial(_mlstm_kernel, heads=heads, dk=dk, dv=dv)
    return pl.pallas_call(
        kern,
        grid=(bsz, s // chunk),
        in_specs=[
            pl.BlockSpec((1, chunk, vc), lambda b, i: (b, i, 0)),
            pl.BlockSpec((1, chunk, vc), lambda b, i: (b, i, 1)),
            pl.BlockSpec((1, chunk, qc), lambda b, i: (b, i, (2 * vc) // qc)),
            pl.BlockSpec((1, qc, chunk), lambda b, i: (b, 0, i)),
            pl.BlockSpec((1, 1, 2 * heads, chunk), lambda b, i: (b, i, 0, 0)),
        ],
        out_specs=pl.BlockSpec((1, chunk, vc), lambda b, i: (b, i, 0)),
        out_shape=jax.ShapeDtypeStruct((bsz, s, vc), BF16),
        scratch_shapes=[
            pltpu.VMEM((heads, dk, dv), F32),
            pltpu.VMEM((heads, dk, LANES), F32),
            pltpu.VMEM((heads, 1, 1), F32),
        ],
        compiler_params=_params(("arbitrary", "arbitrary")),
        name="mlstm",
    )(voq, voq, voq, kt, gates)


def _attn_kernel(q_ref, k_ref, vt_ref, lam_ref, g_ref, bias_ref, o_ref, m_sc, acc_sc, s_sc, *, tq, tk, hd, nh, ahead, lam_init):
    lp = lam_ref[...]
    lam = (jnp.exp(jnp.sum(lp[0:1] * lp[1:2], axis=-1, keepdims=True))
           - jnp.exp(jnp.sum(lp[2:3] * lp[3:4], axis=-1, keepdims=True)) + lam_init)
    g_col = g_ref[...] * (1.0 - lam_init)
    comp_dim = hd // 2
    assert tq == tk and ahead <= nh

    def q_tile(qi, carry):
        q0 = pl.multiple_of(qi * tq, tq)
        qqs = []
        for h in range(nh):
            q = q_ref[0, pl.ds(q0, tq), h * hd:(h + 1) * hd]
            lane = lax.broadcasted_iota(jnp.int32, q.shape, 1)
            zero = jnp.zeros_like(q)
            qqs.append(jnp.concatenate([jnp.where(lane < comp_dim, q, zero), jnp.where(lane >= comp_dim, q, zero)], axis=0))
        m_sc[...] = jnp.full_like(m_sc, -jnp.inf)
        acc_sc[...] = jnp.zeros_like(acc_sc)
        ones = jnp.ones((acc_sc.shape[1] - hd, tk), BF16)

        def scores(k0, h):
            kk = k_ref[0, pl.ds(k0, tk), h * hd:(h + 1) * hd]
            return lax.dot_general(kk, qqs[h], _NT, preferred_element_type=F32)

        def step(j, c):
            k0 = pl.multiple_of(j * tk, tk)
            k_next = pl.multiple_of(jnp.minimum(k0 + tk, k_ref.shape[1] - tk), tk)
            diag = jnp.where(j == qi, 1, 0)
            pending = [s_sc[i] for i in range(ahead)]
            for h in range(nh):
                s = jnp.minimum(pending.pop(0), bias_ref[diag])
                if h + ahead < nh:
                    pending.append(scores(k0, h + ahead))
                else:
                    s_sc[h + ahead - nh] = scores(k_next, h + ahead - nh)
                vt = jnp.concatenate([vt_ref[0, h * hd:(h + 1) * hd, pl.ds(k0, tk)], ones], axis=0)
                m_old = m_sc[h]
                m_new = jnp.maximum(m_old, jnp.max(s, axis=0, keepdims=True))
                alpha = jnp.exp2(m_old - m_new)
                p = jnp.exp2(s - m_new).astype(BF16)
                acc_sc[h] = alpha * acc_sc[h] + jnp.dot(vt, p, preferred_element_type=F32)
                m_sc[h] = m_new
            return c

        for i in range(ahead):
            s_sc[i] = scores(0, i)
        lax.fori_loop(0, qi + 1, step, 0)

        for h in range(nh):
            acc = acc_sc[h]
            acc = acc[:hd] / acc[hd:hd + 1]
            o = acc[:, :tq] - lam * acc[:, tq:]
            o = o * lax.rsqrt(jnp.mean(o * o, axis=0, keepdims=True) + SUBLN_EPS) * g_col
            o_ref[0, pl.ds(q0, tq), h * hd:(h + 1) * hd] = o.T.astype(o_ref.dtype)
        return carry

    lax.fori_loop(0, q_ref.shape[1] // tq, q_tile, 0)


def _attention(q, k, vt, lam_params, subln_g, *, heads, lam_init):
    bsz, s, d = q.shape
    hd = d // heads
    tq = _tile(s, 256)
    tk = tq
    nh = _tile(heads, 8)
    kern = functools.partial(_attn_kernel, tq=tq, tk=tk, hd=hd, nh=nh, ahead=ATTN_AHEAD, lam_init=lam_init)
    kpos = lax.broadcasted_iota(jnp.int32, (tk, 2 * tq), 0)
    qpos = lax.broadcasted_iota(jnp.int32, (tk, 2 * tq), 1) % tq
    bias = jnp.stack([jnp.full((tk, 2 * tq), jnp.inf, F32), jnp.where(kpos <= qpos, jnp.inf, -jnp.inf)])
    return pl.pallas_call(
        kern,
        grid=(bsz, heads // nh),
        in_specs=[
            pl.BlockSpec((1, s, nh * hd), lambda b, h: (b, 0, h)),
            pl.BlockSpec((1, s, nh * hd), lambda b, h: (b, 0, h)),
            pl.BlockSpec((1, nh * hd, s), lambda b, h: (b, h, 0)),
            _resident(lam_params.shape),
            _resident((hd, 1)),
            _resident(bias.shape),
        ],
        out_specs=pl.BlockSpec((1, s, nh * hd), lambda b, h: (b, 0, h)),
        out_shape=jax.ShapeDtypeStruct((bsz, s, d), BF16),
        scratch_shapes=[
            pltpu.VMEM((nh, 1, 2 * tq), F32),
            pltpu.VMEM((nh, hd + BF16_SUBLANES, 2 * tq), F32),
            pltpu.VMEM((ATTN_AHEAD, tk, 2 * tq), F32),
        ],
        compiler_params=_params(("arbitrary", "arbitrary")),
        name="diff_attn",
    )(q, k, vt, lam_params, subln_g.reshape(hd, 1), bias)


def _out_mlp_kernel(x_ref, a_ref, g1_ref, sh_ref, sc_ref, g2_ref, ng_ref, wo_ref, wu_ref, wd_ref, fg_ref, o_ref, *, fc, final):
    x1 = x_ref[0] + g1_ref[0] * jnp.dot(a_ref[0], wo_ref[...], preferred_element_type=F32)
    y = _norm_mod(x1, ng_ref[...], sh_ref[0], sc_ref[0]).astype(BF16)
    acc = jnp.zeros_like(x1)
    for c in range(wu_ref.shape[1] // fc):
        h = jnp.dot(y, wu_ref[:, c * fc:(c + 1) * fc], preferred_element_type=F32)
        h = jnp.square(jnp.maximum(h, 0.0)).astype(BF16)
        acc = acc + jnp.dot(h, wd_ref[c * fc:(c + 1) * fc, :], preferred_element_type=F32)
    x2 = x1 + g2_ref[0] * acc
    if final:
        x2 = _rms(x2, NORM_EPS) * fg_ref[...]
    o_ref[0] = x2


def _out_mlp(x, a, gate1, shift2, scale2, gate2, norm_g, w_out, w_up, w_down, final_g, *, final):
    bsz, s, d = x.shape
    da = a.shape[-1]
    f = w_up.shape[1]
    tm = _tile(s, 512)
    kern = functools.partial(_out_mlp_kernel, fc=_tile(f, 1024), final=final)
    vec = pl.BlockSpec((1, 1, d), lambda b, i: (b, 0, 0))
    return pl.pallas_call(
        kern,
        grid=(bsz, s // tm),
        in_specs=[
            pl.BlockSpec((1, tm, d), lambda b, i: (b, i, 0)),
            pl.BlockSpec((1, tm, da), lambda b, i: (b, i, 0)),
            vec, vec, vec, vec,
            _resident((1, d)),
            _resident((da, d)),
            _resident((d, f)),
            _resident((f, d)),
            _resident((1, d)),
        ],
        out_specs=pl.BlockSpec((1, tm, d), lambda b, i: (b, i, 0)),
        out_shape=jax.ShapeDtypeStruct((bsz, s, d), F32),
        compiler_params=_params(("arbitrary", "arbitrary")),
        name="out_mlp",
    )(x, a, gate1, shift2, scale2, gate2, norm_g.reshape(1, d), w_out, w_up, w_down, final_g.reshape(1, d))


def _lambda_init(layer_number):
    return 0.8 - 0.6 * math.exp(-0.3 * (layer_number - 1))


def kernel(x, c, positions, ada_w, ada_b, norm_mix_g, norm_mlp_g, mlp_w_up, mlp_w_down, mlstm_w_in, mlstm_b_gates, mlstm_head_g, mlstm_w_out, kv_norm_g, kv_ada_w, kv_ada_b, kv_w, diff_w_q, diff_lambda, diff_subln_g, diff_w_out, final_norm_g):
    bsz, s, d = x.shape
    depth = ada_w.shape[0]
    n_a = mlstm_w_in.shape[0]
    a_heads = mlstm_b_gates.shape[-1]
    a_v_cols = mlstm_w_out.shape[1]
    a_q_cols = (mlstm_w_in.shape[-1] - 2 * a_v_cols - 2 * a_heads) // 2
    a_dk, a_dv = a_q_cols // a_heads, a_v_cols // a_heads
    b_comp = diff_lambda.shape[-1]
    b_heads = diff_w_q.shape[-1] // (2 * b_comp)
    rope_dim = b_comp // 4

    mods = _mods(c, ada_w, ada_b).reshape(depth, bsz, 1, 6 * d)
    kv_mods = _mods(c, kv_ada_w[None], kv_ada_b[None]).reshape(bsz, 1, 2 * d)

    k = vt = None
    for layer in range(depth):
        m = mods[layer]
        sh1, sc1, g1, sh2, sc2, g2 = [m[:, :, j * d:(j + 1) * d] for j in range(6)]
        if layer < n_a:
            w_in = mlstm_w_in[layer]
            n_main = 2 * a_q_cols + 2 * a_v_cols
            w_voq = jnp.concatenate([w_in[:, 2 * a_q_cols:n_main], w_in[:, :a_q_cols]], axis=1).astype(BF16)
            w_kt = w_in[:, a_q_cols:2 * a_q_cols].T.astype(BF16)
            w_gates_t = w_in[:, n_main:].T.astype(BF16)
            b_g = mlstm_b_gates[layer].reshape(2 * a_heads, 1)
            chunk = _tile(s, MLSTM_CHUNK)
            voq, kt, gates = _mlstm_proj(x, norm_mix_g[layer], sh1, sc1, w_voq, w_kt, w_gates_t, b_g, mlstm_head_g[layer],
                                         v_cols=a_v_cols, k_scale=a_dk ** -0.5, heads=a_heads, chunk=chunk)
            mix = _mlstm(voq, kt, gates, heads=a_heads, dk=a_dk, dv=a_dv, chunk=chunk)
            w_o = mlstm_w_out[layer]
        else:
            j = layer - n_a
            (q,) = _rope_proj(x, norm_mix_g[layer], sh1, sc1, diff_w_q[j].astype(BF16), positions,
                              out_scale=b_comp ** -0.5 * LOG2E, half=rope_dim // 2)
            mix = _attention(q, k, vt, diff_lambda[j], diff_subln_g[j], heads=b_heads, lam_init=_lambda_init(layer + 1))
            w_o = diff_w_out[j]
        x = _out_mlp(x, mix, g1, sh2, sc2, g2, norm_mlp_g[layer], w_o.astype(BF16), mlp_w_up[layer].astype(BF16),
                     mlp_w_down[layer].astype(BF16), final_norm_g, final=(layer == depth - 1))
        if layer == n_a - 1:
            n_k = b_heads * 2 * b_comp
            k, vt = _rope_proj(x, kv_norm_g, kv_mods[:, :, :d], kv_mods[:, :, d:], kv_w[:, :n_k].astype(BF16), positions,
                               kv_w[:, n_k:].T.astype(BF16), out_scale=1.0, half=rope_dim // 2)
    return x
```

```python
import functools
import math

import jax
import jax.numpy as jnp
from jax import lax
from jax.experimental import pallas as pl
from jax.experimental.pallas import tpu as pltpu

F32 = jnp.float32
BF16 = jnp.bfloat16

NORM_EPS = 1e-6
SUBLN_EPS = 1e-5
ROPE_THETA = 500000.0
ATTN_AHEAD = 2
MLSTM_CHUNK = 256

V7X_VMEM_LIMIT_BYTES = 52 * 1024 * 1024
LANES = 128
BF16_SUBLANES = 16
LOG2E = math.log2(math.e)

_NT = (((1,), (1,)), ((), ()))
_TN = (((0,), (0,)), ((), ()))


def _tile(n, pref):
    t = min(n, pref)
    assert n % t == 0, (n, t)
    return t


def _params(sem):
    return pltpu.CompilerParams(dimension_semantics=sem, vmem_limit_bytes=V7X_VMEM_LIMIT_BYTES)


def _resident(shape):
    nd = len(shape)
    return pl.BlockSpec(shape, lambda *_: (0,) * nd, pipeline_mode=pl.Buffered(1))


def _rms(x, eps):
    return x * lax.rsqrt(jnp.mean(x * x, axis=-1, keepdims=True) + eps)


def _norm_mod(x, g, shift, scale):
    return (_rms(x, NORM_EPS) * g) * (1.0 + scale) + shift


def _mods_kernel(c_ref, w_ref, b_ref, o_ref):
    c = c_ref[...]
    act = (c * jax.nn.sigmoid(c)).astype(BF16)
    o_ref[0] = jnp.dot(act, w_ref[0].astype(BF16), preferred_element_type=F32) + b_ref[0]


def _mods(c, w, b):
    nl, d, n = w.shape
    bsz = c.shape[0]
    tn = _tile(n, 1024)
    return pl.pallas_call(
        _mods_kernel,
        grid=(nl, n // tn),
        in_specs=[
            pl.BlockSpec((bsz, d), lambda l, j: (0, 0)),
            pl.BlockSpec((1, d, tn), lambda l, j: (l, 0, j)),
            pl.BlockSpec((1, 1, tn), lambda l, j: (l, 0, j)),
        ],
        out_specs=pl.BlockSpec((1, bsz, tn), lambda l, j: (l, 0, j)),
        out_shape=jax.ShapeDtypeStruct((nl, bsz, n), F32),
        compiler_params=_params(("arbitrary", "arbitrary")),
        name="mods",
    )(c, w, b.reshape(nl, 1, n))


def _rope_tables(pos_row, half):
    comp_dim = LANES // 2
    tm = pos_row.shape[1]
    fidx = lax.broadcasted_iota(jnp.int32, (half, 1), 0).astype(F32)
    inv_freq = jnp.exp(fidx * (-math.log(ROPE_THETA) / half))
    ang = inv_freq * pos_row.astype(F32)
    rows = 4 * half
    r_idx = lax.broadcasted_iota(jnp.int32, (rows, LANES), 0)
    lane = lax.broadcasted_iota(jnp.int32, (rows, LANES), 1) % comp_dim
    hit = (r_idx < 3 * half) & (r_idx % half == lane % half)
    sel_cos = jnp.where(hit & (lane < 2 * half), 1.0, 0.0).astype(BF16)
    sel_sin = jnp.where(hit & (lane < half), -1.0, jnp.where(hit & (lane < 2 * half), 1.0, 0.0)).astype(BF16)
    lane1 = lax.broadcasted_iota(jnp.int32, (1, LANES), 1) % comp_dim
    ones_elsewhere = jnp.where(lane1 < 2 * half, 0.0, 1.0)

    def expand(v, sel):
        stacked = jnp.concatenate(list(_split3(v)) + [jnp.zeros((half, tm), BF16)], axis=0)
        return lax.dot_general(stacked, sel, _TN, preferred_element_type=F32)

    return expand(jnp.cos(ang), sel_cos) + ones_elsewhere, expand(jnp.sin(ang), sel_sin)


def _apply_rope(r, cos_t, sin_t, half):
    lane = lax.broadcasted_iota(jnp.int32, cos_t.shape, 1)
    pos = lane % (LANES // 2)
    src = jnp.where(pos < half, lane + half, jnp.where(pos < 2 * half, lane - half, lane))
    outs = []
    for j in range(r.shape[1] // LANES):
        blk = r[:, j * LANES:(j + 1) * LANES]
        partner = jnp.take_along_axis(blk, src, axis=1)
        outs.append(blk * cos_t + partner * sin_t)
    return outs


def _rope_proj_kernel(x_ref, g_ref, sh_ref, sc_ref, w_ref, pos_ref, *rest, half, out_scale, nc, with_vt):
    y = _norm_mod(x_ref[0], g_ref[...], sh_ref[0], sc_ref[0]).astype(BF16)
    cos_t, sin_t = _rope_tables(pos_ref[0, 0], half)
    o_ref = rest[-2] if with_vt else rest[-1]
    for c in range(o_ref.shape[-1] // nc):
        r = jnp.dot(y, w_ref[:, c * nc:(c + 1) * nc], preferred_element_type=F32)
        for j, blk in enumerate(_apply_rope(r, cos_t, sin_t, half)):
            if out_scale != 1.0:
                blk = blk * out_scale
            o_ref[0, :, c * nc + j * LANES:c * nc + (j + 1) * LANES] = blk.astype(o_ref.dtype)
    if with_vt:
        wvt_ref, vt_ref = rest[0], rest[-1]
        for c in range(vt_ref.shape[1] // nc):
            r = lax.dot_general(wvt_ref[c * nc:(c + 1) * nc, :], y, _NT, preferred_element_type=F32)
            vt_ref[0, c * nc:(c + 1) * nc, :] = r.astype(vt_ref.dtype)


def _rope_proj(x, g, shift, scale, w, positions, w_vt=None, *, out_scale, half):
    bsz, s, d = x.shape
    n = w.shape[1]
    tm = _tile(s, 512)
    nt = s // tm
    with_vt = w_vt is not None
    kern = functools.partial(_rope_proj_kernel, half=half, out_scale=out_scale, nc=_tile(n, 512), with_vt=with_vt)
    vec = pl.BlockSpec((1, 1, d), lambda b, i: (b, 0, 0))
    pos = pl.BlockSpec((1, 1, 1, tm), lambda b, i: (b, i, 0, 0))
    in_specs = [pl.BlockSpec((1, tm, d), lambda b, i: (b, i, 0)), _resident((1, d)), vec, vec, _resident((d, n)), pos]
    out_specs = [pl.BlockSpec((1, tm, n), lambda b, i: (b, i, 0))]
    out_shape = [jax.ShapeDtypeStruct((bsz, s, n), BF16)]
    args = [x, g.reshape(1, d), shift, scale, w, positions.reshape(bsz, nt, 1, tm)]
    if with_vt:
        nv = w_vt.shape[0]
        in_specs.append(_resident((nv, d)))
        out_specs.append(pl.BlockSpec((1, nv, tm), lambda b, i: (b, 0, i)))
        out_shape.append(jax.ShapeDtypeStruct((bsz, nv, s), BF16))
        args.append(w_vt)
    return pl.pallas_call(
        kern,
        grid=(bsz, nt),
        in_specs=in_specs,
        out_specs=out_specs,
        out_shape=out_shape,
        compiler_params=_params(("arbitrary", "arbitrary")),
        name="rope_proj",
    )(*args)


def _mlstm_proj_kernel(x_ref, g_ref, sh_ref, sc_ref, w_ref, wkt_ref, wg_ref, bg_ref, hg_ref, voq_ref, kt_ref, gates_ref, *, v_cols, k_scale, heads, nc, chunk):
    y = _norm_mod(x_ref[0], g_ref[...], sh_ref[0], sc_ref[0]).astype(BF16)
    for c in range(w_ref.shape[1] // nc):
        col = c * nc
        r = jnp.dot(y, w_ref[:, col:col + nc], preferred_element_type=F32)
        if v_cols <= col < 2 * v_cols:
            r = jax.nn.sigmoid(r) * hg_ref[:, col - v_cols:col - v_cols + nc]
        voq_ref[0, :, col:col + nc] = r.astype(BF16)
    kt = lax.dot_general(wkt_ref[...], y, _NT, preferred_element_type=F32) * k_scale
    kt_ref[0] = kt.astype(BF16)
    gt = lax.dot_general(wg_ref[...], y, _NT, preferred_element_type=F32) + bg_ref[...]
    row = lax.broadcasted_iota(jnp.int32, gt.shape, 0)
    logf = jnp.minimum(gt, 0.0) - jnp.log1p(jnp.exp(-jnp.abs(gt)))
    gt = jnp.where(row < heads, gt, logf)
    for c in range(gt.shape[1] // chunk):
        gates_ref[0, c] = gt[:, c * chunk:(c + 1) * chunk]


def _mlstm_proj(x, g, shift, scale, w_voq, w_kt, w_gates_t, b_gates, head_g, *, v_cols, k_scale, heads, chunk):
    bsz, s, d = x.shape
    n = w_voq.shape[1]
    q_cols = w_kt.shape[0]
    tm = _tile(s, 512)
    kern = functools.partial(_mlstm_proj_kernel, v_cols=v_cols, k_scale=k_scale, heads=heads,
                             nc=_tile(q_cols, 512), chunk=chunk)
    vec = pl.BlockSpec((1, 1, d), lambda b, i: (b, 0, 0))
    return pl.pallas_call(
        kern,
        grid=(bsz, s // tm),
        in_specs=[
            pl.BlockSpec((1, tm, d), lambda b, i: (b, i, 0)),
            _resident((1, d)), vec, vec,
            _resident((d, n)),
            _resident((q_cols, d)),
            _resident((2 * heads, d)),
            _resident((2 * heads, 1)),
            _resident((1, v_cols)),
        ],
        out_specs=[
            pl.BlockSpec((1, tm, n), lambda b, i: (b, i, 0)),
            pl.BlockSpec((1, q_cols, tm), lambda b, i: (b, 0, i)),
            pl.BlockSpec((1, tm // chunk, 2 * heads, chunk), lambda b, i: (b, i, 0, 0)),
        ],
        out_shape=[
            jax.ShapeDtypeStruct((bsz, s, n), BF16),
            jax.ShapeDtypeStruct((bsz, q_cols, s), BF16),
            jax.ShapeDtypeStruct((bsz, s // chunk, 2 * heads, chunk), F32),
        ],
        compiler_params=_params(("arbitrary", "arbitrary")),
        name="mlstm_proj",
    )(x, g.reshape(1, d), shift, scale, w_voq, w_kt, w_gates_t, b_gates, head_g.reshape(1, v_cols))


def _prefix_max(x):
    n = x.shape[1]
    lane = lax.broadcasted_iota(jnp.int32, x.shape, 1)
    shift = 1
    while shift < n:
        x = jnp.maximum(x, jnp.where(lane >= shift, pltpu.roll(x, shift, axis=1), -jnp.inf))
        shift *= 2
    return x


def _split3(x):
    hi = x.astype(BF16)
    r1 = x - hi.astype(F32)
    mid = r1.astype(BF16)
    lo = (r1 - mid.astype(F32)).astype(BF16)
    return hi, mid, lo


def _mlstm_kernel(v_ref, og_ref, q_ref, kt_ref, gates_ref, gates_nb_ref, out_ref, c_sc, n_sc, m_sc, rows_sc, cols_sc, *, heads, dk, dv):
    H = heads
    L = q_ref.shape[1]
    step_i = pl.program_id(1)

    row = lax.broadcasted_iota(jnp.int32, (L, L), 0)
    col = lax.broadcasted_iota(jnp.int32, (L, L), 1)
    causal = col <= row
    upper = (row <= col).astype(F32)

    def gate_rows(gr):
        g = jnp.dot(gr, upper, precision=lax.Precision.HIGHEST, preferred_element_type=F32)[H:]
        return gr[:H] - g, g

    def gate_cols(b, g):
        rows = jnp.concatenate([_prefix_max(b), g], axis=0)
        stacked = jnp.concatenate(list(_split3(rows)) + [jnp.zeros_like(rows, dtype=BF16)], axis=0)
        sel_r = lax.broadcasted_iota(jnp.int32, (8 * H, 2 * H * LANES), 0) % (2 * H)
        sel_c = lax.broadcasted_iota(jnp.int32, (8 * H, 2 * H * LANES), 1) // LANES
        return lax.dot_general(stacked, (sel_r == sel_c).astype(BF16), _TN, preferred_element_type=F32)

    @pl.when(step_i == 0)
    def _():
        c_sc[...] = jnp.zeros_like(c_sc)
        n_sc[...] = jnp.zeros_like(n_sc)
        m_sc[...] = jnp.zeros_like(m_sc)

    @pl.when((step_i == 0) & (pl.program_id(0) == 0))
    def _():
        b0, g0 = gate_rows(gates_ref[0, 0])
        rows_sc[...] = jnp.concatenate([b0, g0], axis=0)
        cols_sc[...] = gate_cols(b0, g0)

    ones = jnp.ones((L, LANES), BF16)
    ones_v = jnp.ones((dv, LANES), BF16)
    rep_l = L // LANES
    rep_v = dv // LANES
    rep_k = dk // LANES

    hs = range(H)
    m_old = [m_sc[h] for h in hs]
    c_old = [c_sc[h] for h in hs]
    n_old = [n_sc[h] for h in hs]
    qh = [q_ref[0, :, h * dk:(h + 1) * dk] for h in hs]
    kth = [kt_ref[0, h * dk:(h + 1) * dk, :] for h in hs]
    vh = [v_ref[0, :, h * dv:(h + 1) * dv] for h in hs]
    rows = rows_sc[...]
    b, g = rows[:H], rows[H:]
    cols = cols_sc[...]
    last = step_i == pl.num_programs(1) - 1
    gr_next = jnp.where(last, gates_nb_ref[0, 0], gates_ref[0, jnp.minimum(step_i + 1, pl.num_programs(1) - 1)])
    b_next, g_next = gate_rows(gr_next)

    s = [jnp.dot(qh[h], kth[h], preferred_element_type=F32) for h in hs]
    b_row = [b[h:h + 1] for h in hs]
    m_end = [jnp.maximum(m_old[h], jnp.max(b_row[h], axis=-1, keepdims=True)) for h in hs]
    c_new, n_new = [], []
    for h in hs:
        w_old = jnp.exp(m_old[h] - m_end[h])
        kw = (kth[h].astype(F32) * jnp.exp(b_row[h] - m_end[h])).astype(BF16)
        c_new.append(w_old * c_old[h] + jnp.dot(kw, vh[h], preferred_element_type=F32))
        n_new.append(w_old * n_old[h] + jnp.dot(kw, ones, preferred_element_type=F32))

    m_c = [jnp.maximum(cols[:, h * LANES:(h + 1) * LANES], m_old[h]) for h in hs]

    num, den = [], []
    for h in hs:
        w = jnp.exp(jnp.where(causal, b_row[h] - jnp.tile(m_c[h], (1, rep_l)), -jnp.inf))
        p = (s[h] * w).astype(BF16)
        w_inter = jnp.exp(m_old[h] - m_c[h])
        qw = (qh[h].astype(F32) * jnp.tile(w_inter, (1, rep_k))).astype(BF16)
        lhs = jnp.concatenate([p, qw], axis=1)
        num.append(jnp.dot(lhs, jnp.concatenate([vh[h], c_old[h].astype(BF16)], axis=0), preferred_element_type=F32))
        den.append(jnp.dot(lhs, jnp.concatenate([ones, n_old[h].astype(BF16)], axis=0), preferred_element_type=F32))

    ssq = [jnp.dot((num[h] * num[h]).astype(BF16), ones_v, preferred_element_type=F32) for h in hs]
    for h in hs:
        g_c = cols[:, (H + h) * LANES:(H + h + 1) * LANES]
        inv = 1.0 / jnp.maximum(jnp.abs(den[h]), jnp.exp(-(g_c + m_c[h])))
        scale = inv * lax.rsqrt(inv * inv * ssq[h] * (1.0 / dv) + NORM_EPS)
        out = num[h] * jnp.tile(scale, (1, rep_v)) * og_ref[0, :, h * dv:(h + 1) * dv].astype(F32)
        out_ref[0, :, h * dv:(h + 1) * dv] = out.astype(out_ref.dtype)

    for h in hs:
        c_sc[h] = c_new[h]
        n_sc[h] = n_new[h]
        m_sc[h] = g[h:h + 1, L - 1:L] + m_end[h]
    cols_sc[...] = gate_cols(b_next, g_next)
    rows_sc[...] = jnp.concatenate([b_next, g_next], axis=0)


def _mlstm(voq, kt, gates, *, heads, dk, dv, chunk):
    bsz, s, _ = voq.shape
    qc, vc = heads * dk, heads * dv
    kern = functools.partial(_mlstm_kernel, heads=heads, dk=dk, dv=dv)
    return pl.pallas_call(
        kern,
        grid=(bsz, s // chunk),
        in_specs=[
            pl.BlockSpec((1, chunk, vc), lambda b, i: (b, i, 0)),
            pl.BlockSpec((1, chunk, vc), lambda b, i: (b, i, 1)),
            pl.BlockSpec((1, chunk, qc), lambda b, i: (b, i, (2 * vc) // qc)),
            pl.BlockSpec((1, qc, chunk), lambda b, i: (b, 0, i)),
            pl.BlockSpec((1, s // chunk, 2 * heads, chunk), lambda b, i: (b, 0, 0, 0)),
            pl.BlockSpec((1, 1, 2 * heads, chunk), lambda b, i: (jnp.minimum(b + 1, bsz - 1), 0, 0, 0)),
        ],
        out_specs=pl.BlockSpec((1, chunk, vc), lambda b, i: (b, i, 0)),
        out_shape=jax.ShapeDtypeStruct((bsz, s, vc), BF16),
        scratch_shapes=[
            pltpu.VMEM((heads, dk, dv), F32),
            pltpu.VMEM((heads, dk, LANES), F32),
            pltpu.VMEM((heads, 1, 1), F32),
            pltpu.VMEM((2 * heads, chunk), F32),
            pltpu.VMEM((chunk, 2 * heads * LANES), F32),
        ],
        compiler_params=_params(("arbitrary", "arbitrary")),
        name="mlstm",
    )(voq, voq, voq, kt, gates, gates)


def _attn_kernel(q_ref, k_ref, vt_ref, lam_ref, g_ref, cap_ref, o_ref, m_sc, acc_sc, s_sc, *, tq, tk, hd, nh, ahead, lam_init):
    lp = lam_ref[...]
    lam = (jnp.exp(jnp.sum(lp[0:1] * lp[1:2], axis=-1, keepdims=True))
           - jnp.exp(jnp.sum(lp[2:3] * lp[3:4], axis=-1, keepdims=True)) + lam_init)
    g_col = g_ref[...] * (1.0 - lam_init)
    comp_dim = hd // 2
    assert tq == tk and ahead <= nh

    def q_tile(qi, carry):
        q0 = pl.multiple_of(qi * tq, tq)
        qqs = []
        for h in range(nh):
            q = q_ref[0, pl.ds(q0, tq), h * hd:(h + 1) * hd]
            lane = lax.broadcasted_iota(jnp.int32, q.shape, 1)
            zero = jnp.zeros_like(q)
            qqs.append(jnp.concatenate([jnp.where(lane < comp_dim, q, zero), jnp.where(lane >= comp_dim, q, zero)], axis=0))
        m_sc[...] = jnp.full_like(m_sc, -jnp.inf)
        acc_sc[...] = jnp.zeros_like(acc_sc)
        ones = jnp.ones((acc_sc.shape[1] - hd, tk), BF16)

        def scores(k0, h):
            kk = k_ref[0, pl.ds(k0, tk), h * hd:(h + 1) * hd]
            return lax.dot_general(kk, qqs[h], _NT, preferred_element_type=F32)

        def step(j, c, diagonal):
            k0 = pl.multiple_of(j * tk, tk)
            pending = [s_sc[i] for i in range(ahead)]
            for h in range(nh):
                s = pending.pop(0)
                if diagonal:
                    s = jnp.minimum(s, cap_ref[...])
                if h + ahead < nh:
                    pending.append(scores(k0, h + ahead))
                elif not diagonal:
                    s_sc[h + ahead - nh] = scores(k0 + tk, h + ahead - nh)
                vt = jnp.concatenate([vt_ref[0, h * hd:(h + 1) * hd, pl.ds(k0, tk)], ones], axis=0)
                m_old = m_sc[h]
                m_new = jnp.maximum(m_old, jnp.max(s, axis=0, keepdims=True))
                alpha = jnp.exp2(m_old - m_new)
                p = jnp.exp2(s - m_new).astype(BF16)
                acc_sc[h] = alpha * acc_sc[h] + jnp.dot(vt, p, preferred_element_type=F32)
                m_sc[h] = m_new
            return c

        for i in range(ahead):
            s_sc[i] = scores(0, i)
        lax.fori_loop(0, qi, functools.partial(step, diagonal=False), 0)
        lax.fori_loop(qi, qi + 1, functools.partial(step, diagonal=True), 0)

        for h in range(nh):
            acc = acc_sc[h]
            acc = acc[:hd] / acc[hd:hd + 1]
            o = acc[:, :tq] - lam * acc[:, tq:]
            o = o * lax.rsqrt(jnp.mean(o * o, axis=0, keepdims=True) + SUBLN_EPS) * g_col
            o_ref[0, pl.ds(q0, tq), h * hd:(h + 1) * hd] = o.T.astype(o_ref.dtype)
        return carry

    lax.fori_loop(0, q_ref.shape[1] // tq, q_tile, 0)


def _attention(q, k, vt, lam_params, subln_g, *, heads, lam_init):
    bsz, s, d = q.shape
    hd = d // heads
    tq = _tile(s, 256)
    tk = tq
    nh = _tile(heads, 8)
    kern = functools.partial(_attn_kernel, tq=tq, tk=tk, hd=hd, nh=nh, ahead=ATTN_AHEAD, lam_init=lam_init)
    kpos = lax.broadcasted_iota(jnp.int32, (tk, 2 * tq), 0)
    qpos = lax.broadcasted_iota(jnp.int32, (tk, 2 * tq), 1) % tq
    cap = jnp.where(kpos <= qpos, jnp.inf, -jnp.inf).astype(F32)
    return pl.pallas_call(
        kern,
        grid=(bsz, heads // nh),
        in_specs=[
            pl.BlockSpec((1, s, nh * hd), lambda b, h: (b, 0, h)),
            pl.BlockSpec((1, s, nh * hd), lambda b, h: (b, 0, h)),
            pl.BlockSpec((1, nh * hd, s), lambda b, h: (b, h, 0)),
            _resident(lam_params.shape),
            _resident((hd, 1)),
            _resident(cap.shape),
        ],
        out_specs=pl.BlockSpec((1, s, nh * hd), lambda b, h: (b, 0, h)),
        out_shape=jax.ShapeDtypeStruct((bsz, s, d), BF16),
        scratch_shapes=[
            pltpu.VMEM((nh, 1, 2 * tq), F32),
            pltpu.VMEM((nh, hd + BF16_SUBLANES, 2 * tq), F32),
            pltpu.VMEM((ATTN_AHEAD, tk, 2 * tq), F32),
        ],
        compiler_params=_params(("arbitrary", "arbitrary")),
        name="diff_attn",
    )(q, k, vt, lam_params, subln_g.reshape(hd, 1), cap)


def _out_mlp_kernel(x_ref, a_ref, g1_ref, sh_ref, sc_ref, g2_ref, ng_ref, wo_ref, wu_ref, wd_ref, fg_ref, o_ref, *, fc, final):
    x1 = x_ref[0] + g1_ref[0] * jnp.dot(a_ref[0], wo_ref[...], preferred_element_type=F32)
    y = _norm_mod(x1, ng_ref[...], sh_ref[0], sc_ref[0]).astype(BF16)
    acc = jnp.zeros_like(x1)
    for c in range(wu_ref.shape[1] // fc):
        h = jnp.dot(y, wu_ref[:, c * fc:(c + 1) * fc], preferred_element_type=F32)
        h = jnp.square(jnp.maximum(h, 0.0)).astype(BF16)
        acc = acc + jnp.dot(h, wd_ref[c * fc:(c + 1) * fc, :], preferred_element_type=F32)
    x2 = x1 + g2_ref[0] * acc
    if final:
        x2 = _rms(x2, NORM_EPS) * fg_ref[...]
    o_ref[0] = x2


def _out_mlp(x, a, gate1, shift2, scale2, gate2, norm_g, w_out, w_up, w_down, final_g, *, final):
    bsz, s, d = x.shape
    da = a.shape[-1]
    f = w_up.shape[1]
    tm = _tile(s, 512)
    kern = functools.partial(_out_mlp_kernel, fc=_tile(f, 1024), final=final)
    vec = pl.BlockSpec((1, 1, d), lambda b, i: (b, 0, 0))
    return pl.pallas_call(
        kern,
        grid=(bsz, s // tm),
        in_specs=[
            pl.BlockSpec((1, tm, d), lambda b, i: (b, i, 0)),
            pl.BlockSpec((1, tm, da), lambda b, i: (b, i, 0)),
            vec, vec, vec, vec,
            _resident((1, d)),
            _resident((da, d)),
            _resident((d, f)),
            _resident((f, d)),
            _resident((1, d)),
        ],
        out_specs=pl.BlockSpec((1, tm, d), lambda b, i: (b, i, 0)),
        out_shape=jax.ShapeDtypeStruct((bsz, s, d), F32),
        compiler_params=_params(("arbitrary", "arbitrary")),
        name="out_mlp",
    )(x, a, gate1, shift2, scale2, gate2, norm_g.reshape(1, d), w_out, w_up, w_down, final_g.reshape(1, d))


def _lambda_init(layer_number):
    return 0.8 - 0.6 * math.exp(-0.3 * (layer_number - 1))


def kernel(x, c, positions, ada_w, ada_b, norm_mix_g, norm_mlp_g, mlp_w_up, mlp_w_down, mlstm_w_in, mlstm_b_gates, mlstm_head_g, mlstm_w_out, kv_norm_g, kv_ada_w, kv_ada_b, kv_w, diff_w_q, diff_lambda, diff_subln_g, diff_w_out, final_norm_g):
    bsz, s, d = x.shape
    depth = ada_w.shape[0]
    n_a = mlstm_w_in.shape[0]
    a_heads = mlstm_b_gates.shape[-1]
    a_v_cols = mlstm_w_out.shape[1]
    a_q_cols = (mlstm_w_in.shape[-1] - 2 * a_v_cols - 2 * a_heads) // 2
    a_dk, a_dv = a_q_cols // a_heads, a_v_cols // a_heads
    b_comp = diff_lambda.shape[-1]
    b_heads = diff_w_q.shape[-1] // (2 * b_comp)
    rope_dim = b_comp // 4

    mods = _mods(c, ada_w, ada_b).reshape(depth, bsz, 1, 6 * d)
    kv_mods = _mods(c, kv_ada_w[None], kv_ada_b[None]).reshape(bsz, 1, 2 * d)

    k = vt = None
    for layer in range(depth):
        m = mods[layer]
        sh1, sc1, g1, sh2, sc2, g2 = [m[:, :, j * d:(j + 1) * d] for j in range(6)]
        if layer < n_a:
            w_in = mlstm_w_in[layer]
            n_main = 2 * a_q_cols + 2 * a_v_cols
            w_voq = jnp.concatenate([w_in[:, 2 * a_q_cols:n_main], w_in[:, :a_q_cols]], axis=1).astype(BF16)
            w_kt = w_in[:, a_q_cols:2 * a_q_cols].T.astype(BF16)
            w_gates_t = w_in[:, n_main:].T.astype(BF16)
            b_g = mlstm_b_gates[layer].reshape(2 * a_heads, 1)
            chunk = _tile(s, MLSTM_CHUNK)
            voq, kt, gates = _mlstm_proj(x, norm_mix_g[layer], sh1, sc1, w_voq, w_kt, w_gates_t, b_g, mlstm_head_g[layer],
                                         v_cols=a_v_cols, k_scale=a_dk ** -0.5, heads=a_heads, chunk=chunk)
            mix = _mlstm(voq, kt, gates, heads=a_heads, dk=a_dk, dv=a_dv, chunk=chunk)
            w_o = mlstm_w_out[layer]
        else:
            j = layer - n_a
            (q,) = _rope_proj(x, norm_mix_g[layer], sh1, sc1, diff_w_q[j].astype(BF16), positions,
                              out_scale=b_comp ** -0.5 * LOG2E, half=rope_dim // 2)
            mix = _attention(q, k, vt, diff_lambda[j], diff_subln_g[j], heads=b_heads, lam_init=_lambda_init(layer + 1))
            w_o = diff_w_out[j]
        x = _out_mlp(x, mix, g1, sh2, sc2, g2, norm_mlp_g[layer], w_o.astype(BF16), mlp_w_up[layer].astype(BF16),
                     mlp_w_down[layer].astype(BF16), final_norm_g, final=(layer == depth - 1))
        if layer == n_a - 1:
            n_k = b_heads * 2 * b_comp
            k, vt = _rope_proj(x, kv_norm_g, kv_mods[:, :, :d], kv_mods[:, :, d:], kv_w[:, :n_k].astype(BF16), positions,
                               kv_w[:, n_k:].T.astype(BF16), out_scale=1.0, half=rope_dim // 2)
    return x
```

```python
import functools
import math

import jax
import jax.numpy as jnp
from jax import lax
from jax.experimental import pallas as pl
from jax.experimental.pallas import tpu as pltpu

F32 = jnp.float32
BF16 = jnp.bfloat16

NORM_EPS = 1e-6
SUBLN_EPS = 1e-5
ROPE_THETA = 500000.0
PROJ_TOKENS = 1024
ATTN_AHEAD = 2
MLSTM_CHUNK = 256

V7X_VMEM_LIMIT_BYTES = 52 * 1024 * 1024
LANES = 128
BF16_SUBLANES = 16
LOG2E = math.log2(math.e)

_NT = (((1,), (1,)), ((), ()))
_TN = (((0,), (0,)), ((), ()))


def _tile(n, pref):
    t = min(n, pref)
    assert n % t == 0, (n, t)
    return t


def _params(sem):
    return pltpu.CompilerParams(dimension_semantics=sem, vmem_limit_bytes=V7X_VMEM_LIMIT_BYTES)


def _resident(shape):
    nd = len(shape)
    return pl.BlockSpec(shape, lambda *_: (0,) * nd, pipeline_mode=pl.Buffered(1))


def _rms(x, eps):
    return x * lax.rsqrt(jnp.mean(x * x, axis=-1, keepdims=True) + eps)


def _norm_mod(x, g, shift, scale):
    return (_rms(x, NORM_EPS) * g) * (1.0 + scale) + shift


def _mods_kernel(c_ref, w_ref, b_ref, o_ref):
    c = c_ref[...]
    act = (c * jax.nn.sigmoid(c)).astype(BF16)
    o_ref[0] = jnp.dot(act, w_ref[0].astype(BF16), preferred_element_type=F32) + b_ref[0]


def _mods(c, w, b):
    nl, d, n = w.shape
    bsz = c.shape[0]
    tn = _tile(n, 1024)
    return pl.pallas_call(
        _mods_kernel,
        grid=(nl, n // tn),
        in_specs=[
            pl.BlockSpec((bsz, d), lambda l, j: (0, 0)),
            pl.BlockSpec((1, d, tn), lambda l, j: (l, 0, j)),
            pl.BlockSpec((1, 1, tn), lambda l, j: (l, 0, j)),
        ],
        out_specs=pl.BlockSpec((1, bsz, tn), lambda l, j: (l, 0, j)),
        out_shape=jax.ShapeDtypeStruct((nl, bsz, n), F32),
        compiler_params=_params(("arbitrary", "arbitrary")),
        name="mods",
    )(c, w, b.reshape(nl, 1, n))


def _rope_tables(pos_row, half):
    comp_dim = LANES // 2
    tm = pos_row.shape[1]
    fidx = lax.broadcasted_iota(jnp.int32, (half, 1), 0).astype(F32)
    inv_freq = jnp.exp(fidx * (-math.log(ROPE_THETA) / half))
    ang = inv_freq * pos_row.astype(F32)
    rows = 4 * half
    r_idx = lax.broadcasted_iota(jnp.int32, (rows, LANES), 0)
    lane = lax.broadcasted_iota(jnp.int32, (rows, LANES), 1) % comp_dim
    hit = (r_idx < 3 * half) & (r_idx % half == lane % half)
    sel_cos = jnp.where(hit & (lane < 2 * half), 1.0, 0.0).astype(BF16)
    sel_sin = jnp.where(hit & (lane < half), -1.0, jnp.where(hit & (lane < 2 * half), 1.0, 0.0)).astype(BF16)
    lane1 = lax.broadcasted_iota(jnp.int32, (1, LANES), 1) % comp_dim
    ones_elsewhere = jnp.where(lane1 < 2 * half, 0.0, 1.0)

    def expand(v, sel):
        stacked = jnp.concatenate(list(_split3(v)) + [jnp.zeros((half, tm), BF16)], axis=0)
        return lax.dot_general(stacked, sel, _TN, preferred_element_type=F32)

    return expand(jnp.cos(ang), sel_cos) + ones_elsewhere, expand(jnp.sin(ang), sel_sin)


def _apply_rope(r, cos_t, sin_t, half):
    lane = lax.broadcasted_iota(jnp.int32, cos_t.shape, 1)
    pos = lane % (LANES // 2)
    src = jnp.where(pos < half, lane + half, jnp.where(pos < 2 * half, lane - half, lane))
    outs = []
    for j in range(r.shape[1] // LANES):
        blk = r[:, j * LANES:(j + 1) * LANES]
        partner = jnp.take_along_axis(blk, src, axis=1)
        outs.append(blk * cos_t + partner * sin_t)
    return outs


def _rope_proj_kernel(x_ref, g_ref, sh_ref, sc_ref, w_ref, pos_ref, *rest, half, out_scale, nc, with_vt):
    y = _norm_mod(x_ref[0], g_ref[...], sh_ref[0], sc_ref[0]).astype(BF16)
    cos_t, sin_t = _rope_tables(pos_ref[0, 0], half)
    o_ref = rest[-2] if with_vt else rest[-1]
    for c in range(o_ref.shape[-1] // nc):
        r = jnp.dot(y, w_ref[:, c * nc:(c + 1) * nc], preferred_element_type=F32)
        for j, blk in enumerate(_apply_rope(r, cos_t, sin_t, half)):
            if out_scale != 1.0:
                blk = blk * out_scale
            o_ref[0, :, c * nc + j * LANES:c * nc + (j + 1) * LANES] = blk.astype(o_ref.dtype)
    if with_vt:
        wvt_ref, vt_ref = rest[0], rest[-1]
        for c in range(vt_ref.shape[1] // nc):
            r = lax.dot_general(wvt_ref[c * nc:(c + 1) * nc, :], y, _NT, preferred_element_type=F32)
            vt_ref[0, c * nc:(c + 1) * nc, :] = r.astype(vt_ref.dtype)


def _rope_proj(x, g, shift, scale, w, positions, w_vt=None, *, out_scale, half):
    bsz, s, d = x.shape
    n = w.shape[1]
    tm = _tile(s, PROJ_TOKENS)
    nt = s // tm
    with_vt = w_vt is not None
    kern = functools.partial(_rope_proj_kernel, half=half, out_scale=out_scale, nc=_tile(n, 512), with_vt=with_vt)
    vec = pl.BlockSpec((1, 1, d), lambda b, i: (b, 0, 0))
    pos = pl.BlockSpec((1, 1, 1, tm), lambda b, i: (b, i, 0, 0))
    in_specs = [pl.BlockSpec((1, tm, d), lambda b, i: (b, i, 0)), _resident((1, d)), vec, vec, _resident((d, n)), pos]
    out_specs = [pl.BlockSpec((1, tm, n), lambda b, i: (b, i, 0))]
    out_shape = [jax.ShapeDtypeStruct((bsz, s, n), BF16)]
    args = [x, g.reshape(1, d), shift, scale, w, positions.reshape(bsz, nt, 1, tm)]
    if with_vt:
        nv = w_vt.shape[0]
        in_specs.append(_resident((nv, d)))
        out_specs.append(pl.BlockSpec((1, nv, tm), lambda b, i: (b, 0, i)))
        out_shape.append(jax.ShapeDtypeStruct((bsz, nv, s), BF16))
        args.append(w_vt)
    return pl.pallas_call(
        kern,
        grid=(bsz, nt),
        in_specs=in_specs,
        out_specs=out_specs,
        out_shape=out_shape,
        compiler_params=_params(("arbitrary", "arbitrary")),
        name="rope_proj",
    )(*args)


def _mlstm_proj_kernel(x_ref, g_ref, sh_ref, sc_ref, w_ref, wkt_ref, wg_ref, bg_ref, hg_ref, voq_ref, kt_ref, gates_ref, *, v_cols, k_scale, heads, nc, chunk):
    y = _norm_mod(x_ref[0], g_ref[...], sh_ref[0], sc_ref[0]).astype(BF16)
    for c in range(w_ref.shape[1] // nc):
        col = c * nc
        r = jnp.dot(y, w_ref[:, col:col + nc], preferred_element_type=F32)
        if v_cols <= col < 2 * v_cols:
            r = jax.nn.sigmoid(r) * hg_ref[:, col - v_cols:col - v_cols + nc]
        voq_ref[0, :, col:col + nc] = r.astype(BF16)
    kt = lax.dot_general(wkt_ref[...], y, _NT, preferred_element_type=F32) * k_scale
    kt_ref[0] = kt.astype(BF16)
    gt = lax.dot_general(wg_ref[...], y, _NT, preferred_element_type=F32) + bg_ref[...]
    row = lax.broadcasted_iota(jnp.int32, gt.shape, 0)
    logf = jnp.minimum(gt, 0.0) - jnp.log1p(jnp.exp(-jnp.abs(gt)))
    gt = jnp.where(row < heads, gt, logf)
    for c in range(gt.shape[1] // chunk):
        gates_ref[0, c] = gt[:, c * chunk:(c + 1) * chunk]


def _mlstm_proj(x, g, shift, scale, w_voq, w_kt, w_gates_t, b_gates, head_g, *, v_cols, k_scale, heads, chunk):
    bsz, s, d = x.shape
    n = w_voq.shape[1]
    q_cols = w_kt.shape[0]
    tm = _tile(s, PROJ_TOKENS)
    kern = functools.partial(_mlstm_proj_kernel, v_cols=v_cols, k_scale=k_scale, heads=heads,
                             nc=_tile(q_cols, 512), chunk=chunk)
    vec = pl.BlockSpec((1, 1, d), lambda b, i: (b, 0, 0))
    return pl.pallas_call(
        kern,
        grid=(bsz, s // tm),
        in_specs=[
            pl.BlockSpec((1, tm, d), lambda b, i: (b, i, 0)),
            _resident((1, d)), vec, vec,
            _resident((d, n)),
            _resident((q_cols, d)),
            _resident((2 * heads, d)),
            _resident((2 * heads, 1)),
            _resident((1, v_cols)),
        ],
        out_specs=[
            pl.BlockSpec((1, tm, n), lambda b, i: (b, i, 0)),
            pl.BlockSpec((1, q_cols, tm), lambda b, i: (b, 0, i)),
            pl.BlockSpec((1, tm // chunk, 2 * heads, chunk), lambda b, i: (b, i, 0, 0)),
        ],
        out_shape=[
            jax.ShapeDtypeStruct((bsz, s, n), BF16),
            jax.ShapeDtypeStruct((bsz, q_cols, s), BF16),
            jax.ShapeDtypeStruct((bsz, s // chunk, 2 * heads, chunk), F32),
        ],
        compiler_params=_params(("arbitrary", "arbitrary")),
        name="mlstm_proj",
    )(x, g.reshape(1, d), shift, scale, w_voq, w_kt, w_gates_t, b_gates, head_g.reshape(1, v_cols))


def _prefix_max(x):
    n = x.shape[1]
    lane = lax.broadcasted_iota(jnp.int32, x.shape, 1)
    shift = 1
    while shift < n:
        x = jnp.maximum(x, jnp.where(lane >= shift, pltpu.roll(x, shift, axis=1), -jnp.inf))
        shift *= 2
    return x


def _split3(x):
    hi = x.astype(BF16)
    r1 = x - hi.astype(F32)
    mid = r1.astype(BF16)
    lo = (r1 - mid.astype(F32)).astype(BF16)
    return hi, mid, lo


def _mlstm_kernel(v_ref, og_ref, q_ref, kt_ref, gates_ref, gates_nb_ref, out_ref, c_sc, n_sc, m_sc, rows_sc, cols_sc, *, heads, dk, dv):
    H = heads
    L = q_ref.shape[1]
    step_i = pl.program_id(1)

    row = lax.broadcasted_iota(jnp.int32, (L, L), 0)
    col = lax.broadcasted_iota(jnp.int32, (L, L), 1)
    causal = col <= row
    upper = (row <= col).astype(F32)

    def gate_rows(gr):
        g = jnp.dot(gr, upper, precision=lax.Precision.HIGHEST, preferred_element_type=F32)[H:]
        return gr[:H] - g, g

    def gate_cols(b, g):
        rows = jnp.concatenate([_prefix_max(b), g], axis=0)
        stacked = jnp.concatenate(list(_split3(rows)) + [jnp.zeros_like(rows, dtype=BF16)], axis=0)
        sel_r = lax.broadcasted_iota(jnp.int32, (8 * H, 2 * H * LANES), 0) % (2 * H)
        sel_c = lax.broadcasted_iota(jnp.int32, (8 * H, 2 * H * LANES), 1) // LANES
        return lax.dot_general(stacked, (sel_r == sel_c).astype(BF16), _TN, preferred_element_type=F32)

    @pl.when(step_i == 0)
    def _():
        c_sc[...] = jnp.zeros_like(c_sc)
        n_sc[...] = jnp.zeros_like(n_sc)
        m_sc[...] = jnp.zeros_like(m_sc)

    @pl.when((step_i == 0) & (pl.program_id(0) == 0))
    def _():
        b0, g0 = gate_rows(gates_ref[0, 0])
        rows_sc[...] = jnp.concatenate([b0, g0], axis=0)
        cols_sc[...] = gate_cols(b0, g0)

    ones = jnp.ones((L, LANES), BF16)
    ones_v = jnp.ones((dv, LANES), BF16)
    rep_l = L // LANES
    rep_v = dv // LANES
    rep_k = dk // LANES

    hs = range(H)
    m_old = [m_sc[h] for h in hs]
    c_old = [c_sc[h] for h in hs]
    n_old = [n_sc[h] for h in hs]
    qh = [q_ref[0, :, h * dk:(h + 1) * dk] for h in hs]
    kth = [kt_ref[0, h * dk:(h + 1) * dk, :] for h in hs]
    vh = [v_ref[0, :, h * dv:(h + 1) * dv] for h in hs]
    rows = rows_sc[...]
    b, g = rows[:H], rows[H:]
    cols = cols_sc[...]
    last = step_i == pl.num_programs(1) - 1
    gr_next = jnp.where(last, gates_nb_ref[0, 0], gates_ref[0, jnp.minimum(step_i + 1, pl.num_programs(1) - 1)])
    b_next, g_next = gate_rows(gr_next)

    s = [jnp.dot(qh[h], kth[h], preferred_element_type=F32) for h in hs]
    b_row = [b[h:h + 1] for h in hs]
    m_end = [jnp.maximum(m_old[h], jnp.max(b_row[h], axis=-1, keepdims=True)) for h in hs]
    c_new, n_new = [], []
    for h in hs:
        w_old = jnp.exp(m_old[h] - m_end[h])
        kw = (kth[h].astype(F32) * jnp.exp(b_row[h] - m_end[h])).astype(BF16)
        c_new.append(w_old * c_old[h] + jnp.dot(kw, vh[h], preferred_element_type=F32))
        n_new.append(w_old * n_old[h] + jnp.dot(kw, ones, preferred_element_type=F32))

    m_c = [jnp.maximum(cols[:, h * LANES:(h + 1) * LANES], m_old[h]) for h in hs]

    num, den = [], []
    for h in hs:
        w = jnp.exp(jnp.where(causal, b_row[h] - jnp.tile(m_c[h], (1, rep_l)), -jnp.inf))
        p = (s[h] * w).astype(BF16)
        w_inter = jnp.exp(m_old[h] - m_c[h])
        qw = (qh[h].astype(F32) * jnp.tile(w_inter, (1, rep_k))).astype(BF16)
        lhs = jnp.concatenate([p, qw], axis=1)
        num.append(jnp.dot(lhs, jnp.concatenate([vh[h], c_old[h].astype(BF16)], axis=0), preferred_element_type=F32))
        den.append(jnp.dot(lhs, jnp.concatenate([ones, n_old[h].astype(BF16)], axis=0), preferred_element_type=F32))

    ssq = [jnp.dot((num[h] * num[h]).astype(BF16), ones_v, preferred_element_type=F32) for h in hs]
    for h in hs:
        g_c = cols[:, (H + h) * LANES:(H + h + 1) * LANES]
        inv = 1.0 / jnp.maximum(jnp.abs(den[h]), jnp.exp(-(g_c + m_c[h])))
        scale = inv * lax.rsqrt(inv * inv * ssq[h] * (1.0 / dv) + NORM_EPS)
        out = num[h] * jnp.tile(scale, (1, rep_v)) * og_ref[0, :, h * dv:(h + 1) * dv].astype(F32)
        out_ref[0, :, h * dv:(h + 1) * dv] = out.astype(out_ref.dtype)

    for h in hs:
        c_sc[h] = c_new[h]
        n_sc[h] = n_new[h]
        m_sc[h] = g[h:h + 1, L - 1:L] + m_end[h]
    cols_sc[...] = gate_cols(b_next, g_next)
    rows_sc[...] = jnp.concatenate([b_next, g_next], axis=0)


def _mlstm(voq, kt, gates, *, heads, dk, dv, chunk):
    bsz, s, _ = voq.shape
    qc, vc = heads * dk, heads * dv
    kern = functools.partial(_mlstm_kernel, heads=heads, dk=dk, dv=dv)
    return pl.pallas_call(
        kern,
        grid=(bsz, s // chunk),
        in_specs=[
            pl.BlockSpec((1, chunk, vc), lambda b, i: (b, i, 0)),
            pl.BlockSpec((1, chunk, vc), lambda b, i: (b, i, 1)),
            pl.BlockSpec((1, chunk, qc), lambda b, i: (b, i, (2 * vc) // qc)),
            pl.BlockSpec((1, qc, chunk), lambda b, i: (b, 0, i)),
            pl.BlockSpec((1, s // chunk, 2 * heads, chunk), lambda b, i: (b, 0, 0, 0)),
            pl.BlockSpec((1, 1, 2 * heads, chunk), lambda b, i: (jnp.minimum(b + 1, bsz - 1), 0, 0, 0)),
        ],
        out_specs=pl.BlockSpec((1, chunk, vc), lambda b, i: (b, i, 0)),
        out_shape=jax.ShapeDtypeStruct((bsz, s, vc), BF16),
        scratch_shapes=[
            pltpu.VMEM((heads, dk, dv), F32),
            pltpu.VMEM((heads, dk, LANES), F32),
            pltpu.VMEM((heads, 1, 1), F32),
            pltpu.VMEM((2 * heads, chunk), F32),
            pltpu.VMEM((chunk, 2 * heads * LANES), F32),
        ],
        compiler_params=_params(("arbitrary", "arbitrary")),
        name="mlstm",
    )(voq, voq, voq, kt, gates, gates)


def _attn_kernel(q_ref, k_ref, vt_ref, lam_ref, g_ref, cap_ref, o_ref, m_sc, acc_sc, s_sc, qq_sc, *, tq, tk, hd, nh, ahead, lam_init):
    lp = lam_ref[...]
    lam = (jnp.exp(jnp.sum(lp[0:1] * lp[1:2], axis=-1, keepdims=True))
           - jnp.exp(jnp.sum(lp[2:3] * lp[3:4], axis=-1, keepdims=True)) + lam_init)
    g_col = g_ref[...] * (1.0 - lam_init)
    comp_dim = hd // 2
    assert tq == tk and ahead <= nh

    ones = jnp.ones((acc_sc.shape[1] - hd, tk), BF16)
    n_tiles = q_ref.shape[1] // tq

    def scores(k0, h):
        kk = k_ref[0, pl.ds(k0, tk), h * hd:(h + 1) * hd]
        return lax.dot_general(kk, qq_sc[h], _NT, preferred_element_type=F32)

    def prepare_tile(q0):
        for h in range(nh):
            q = q_ref[0, pl.ds(q0, tq), h * hd:(h + 1) * hd]
            lane = lax.broadcasted_iota(jnp.int32, q.shape, 1)
            zero = jnp.zeros_like(q)
            qq_sc[h] = jnp.concatenate([jnp.where(lane < comp_dim, q, zero), jnp.where(lane >= comp_dim, q, zero)], axis=0)
        for i in range(ahead):
            s_sc[i] = scores(0, i)

    prepare_tile(0)
    m_sc[...] = jnp.full_like(m_sc, -jnp.inf)
    acc_sc[...] = jnp.zeros_like(acc_sc)

    def q_tile(qi, carry):
        q0 = pl.multiple_of(qi * tq, tq)


        def step(j, c, diagonal):
            k0 = pl.multiple_of(j * tk, tk)
            pending = [s_sc[i] for i in range(ahead)]
            for h in range(nh):
                s = pending.pop(0)
                if diagonal:
                    s = jnp.minimum(s, cap_ref[...])
                if h + ahead < nh:
                    pending.append(scores(k0, h + ahead))
                elif not diagonal:
                    s_sc[h + ahead - nh] = scores(k0 + tk, h + ahead - nh)
                m_old = m_sc[h]
                m_new = jnp.maximum(m_old, jnp.max(s, axis=0, keepdims=True))
                alpha = jnp.exp2(m_old - m_new)
                p = jnp.exp2(s - m_new).astype(BF16)
                vt = jnp.concatenate([vt_ref[0, h * hd:(h + 1) * hd, pl.ds(k0, tk)], ones], axis=0)
                acc_sc[h] = alpha * acc_sc[h] + jnp.dot(vt, p, preferred_element_type=F32)
                m_sc[h] = m_new
            return c

        lax.fori_loop(0, qi, functools.partial(step, diagonal=False), 0)
        lax.fori_loop(qi, qi + 1, functools.partial(step, diagonal=True), 0)

        prepare_tile(pl.multiple_of(jnp.minimum(qi + 1, n_tiles - 1) * tq, tq))
        for h in range(nh):
            acc = acc_sc[h]
            acc_sc[h] = jnp.zeros_like(acc)
            acc = acc[:hd] / acc[hd:hd + 1]
            o = acc[:, :tq] - lam * acc[:, tq:]
            o = o * lax.rsqrt(jnp.mean(o * o, axis=0, keepdims=True) + SUBLN_EPS) * g_col
            o_ref[0, pl.ds(q0, tq), h * hd:(h + 1) * hd] = o.T.astype(o_ref.dtype)
        m_sc[...] = jnp.full_like(m_sc, -jnp.inf)
        return carry

    lax.fori_loop(0, n_tiles, q_tile, 0)


def _attention(q, k, vt, lam_params, subln_g, *, heads, lam_init):
    bsz, s, d = q.shape
    hd = d // heads
    tq = _tile(s, 256)
    tk = tq
    nh = _tile(heads, 8)
    kern = functools.partial(_attn_kernel, tq=tq, tk=tk, hd=hd, nh=nh, ahead=ATTN_AHEAD, lam_init=lam_init)
    kpos = lax.broadcasted_iota(jnp.int32, (tk, 2 * tq), 0)
    qpos = lax.broadcasted_iota(jnp.int32, (tk, 2 * tq), 1) % tq
    cap = jnp.where(kpos <= qpos, jnp.inf, -jnp.inf).astype(F32)
    return pl.pallas_call(
        kern,
        grid=(bsz, heads // nh),
        in_specs=[
            pl.BlockSpec((1, s, nh * hd), lambda b, h: (b, 0, h)),
            pl.BlockSpec((1, s, nh * hd), lambda b, h: (b, 0, h)),
            pl.BlockSpec((1, nh * hd, s), lambda b, h: (b, h, 0)),
            _resident(lam_params.shape),
            _resident((hd, 1)),
            _resident(cap.shape),
        ],
        out_specs=pl.BlockSpec((1, s, nh * hd), lambda b, h: (b, 0, h)),
        out_shape=jax.ShapeDtypeStruct((bsz, s, d), BF16),
        scratch_shapes=[
            pltpu.VMEM((nh, 1, 2 * tq), F32),
            pltpu.VMEM((nh, hd + BF16_SUBLANES, 2 * tq), F32),
            pltpu.VMEM((ATTN_AHEAD, tk, 2 * tq), F32),
            pltpu.VMEM((nh, 2 * tq, hd), BF16),
        ],
        compiler_params=_params(("arbitrary", "arbitrary")),
        name="diff_attn",
    )(q, k, vt, lam_params, subln_g.reshape(hd, 1), cap)


def _out_mlp_kernel(x_ref, a_ref, g1_ref, sh_ref, sc_ref, g2_ref, ng_ref, wo_ref, wu_ref, wd_ref, fg_ref, o_ref, *, fc, final):
    x1 = x_ref[0] + g1_ref[0] * jnp.dot(a_ref[0], wo_ref[...], preferred_element_type=F32)
    y = _norm_mod(x1, ng_ref[...], sh_ref[0], sc_ref[0]).astype(BF16)
    acc = jnp.zeros_like(x1)
    for c in range(wu_ref.shape[1] // fc):
        h = jnp.dot(y, wu_ref[:, c * fc:(c + 1) * fc], preferred_element_type=F32)
        h = jnp.square(jnp.maximum(h, 0.0)).astype(BF16)
        acc = acc + jnp.dot(h, wd_ref[c * fc:(c + 1) * fc, :], preferred_element_type=F32)
    x2 = x1 + g2_ref[0] * acc
    if final:
        x2 = _rms(x2, NORM_EPS) * fg_ref[...]
    o_ref[0] = x2


def _out_mlp(x, a, gate1, shift2, scale2, gate2, norm_g, w_out, w_up, w_down, final_g, *, final):
    bsz, s, d = x.shape
    da = a.shape[-1]
    f = w_up.shape[1]
    tm = _tile(s, 512)
    kern = functools.partial(_out_mlp_kernel, fc=_tile(f, 1024), final=final)
    vec = pl.BlockSpec((1, 1, d), lambda b, i: (b, 0, 0))
    return pl.pallas_call(
        kern,
        grid=(bsz, s // tm),
        in_specs=[
            pl.BlockSpec((1, tm, d), lambda b, i: (b, i, 0)),
            pl.BlockSpec((1, tm, da), lambda b, i: (b, i, 0)),
            vec, vec, vec, vec,
            _resident((1, d)),
            _resident((da, d)),
            _resident((d, f)),
            _resident((f, d)),
            _resident((1, d)),
        ],
        out_specs=pl.BlockSpec((1, tm, d), lambda b, i: (b, i, 0)),
        out_shape=jax.ShapeDtypeStruct((bsz, s, d), F32),
        compiler_params=_params(("arbitrary", "arbitrary")),
        name="out_mlp",
    )(x, a, gate1, shift2, scale2, gate2, norm_g.reshape(1, d), w_out, w_up, w_down, final_g.reshape(1, d))


def _lambda_init(layer_number):
    return 0.8 - 0.6 * math.exp(-0.3 * (layer_number - 1))


def kernel(x, c, positions, ada_w, ada_b, norm_mix_g, norm_mlp_g, mlp_w_up, mlp_w_down, mlstm_w_in, mlstm_b_gates, mlstm_head_g, mlstm_w_out, kv_norm_g, kv_ada_w, kv_ada_b, kv_w, diff_w_q, diff_lambda, diff_subln_g, diff_w_out, final_norm_g):
    bsz, s, d = x.shape
    depth = ada_w.shape[0]
    n_a = mlstm_w_in.shape[0]
    a_heads = mlstm_b_gates.shape[-1]
    a_v_cols = mlstm_w_out.shape[1]
    a_q_cols = (mlstm_w_in.shape[-1] - 2 * a_v_cols - 2 * a_heads) // 2
    a_dk, a_dv = a_q_cols // a_heads, a_v_cols // a_heads
    b_comp = diff_lambda.shape[-1]
    b_heads = diff_w_q.shape[-1] // (2 * b_comp)
    rope_dim = b_comp // 4

    mods = _mods(c, ada_w, ada_b).reshape(depth, bsz, 1, 6 * d)
    kv_mods = _mods(c, kv_ada_w[None], kv_ada_b[None]).reshape(bsz, 1, 2 * d)

    k = vt = None
    for layer in range(depth):
        m = mods[layer]
        sh1, sc1, g1, sh2, sc2, g2 = [m[:, :, j * d:(j + 1) * d] for j in range(6)]
        if layer < n_a:
            w_in = mlstm_w_in[layer]
            n_main = 2 * a_q_cols + 2 * a_v_cols
            w_voq = jnp.concatenate([w_in[:, 2 * a_q_cols:n_main], w_in[:, :a_q_cols]], axis=1).astype(BF16)
            w_kt = w_in[:, a_q_cols:2 * a_q_cols].T.astype(BF16)
            w_gates_t = w_in[:, n_main:].T.astype(BF16)
            b_g = mlstm_b_gates[layer].reshape(2 * a_heads, 1)
            chunk = _tile(s, MLSTM_CHUNK)
            voq, kt, gates = _mlstm_proj(x, norm_mix_g[layer], sh1, sc1, w_voq, w_kt, w_gates_t, b_g, mlstm_head_g[layer],
                                         v_cols=a_v_cols, k_scale=a_dk ** -0.5, heads=a_heads, chunk=chunk)
            mix = _mlstm(voq, kt, gates, heads=a_heads, dk=a_dk, dv=a_dv, chunk=chunk)
            w_o = mlstm_w_out[layer]
        else:
            j = layer - n_a
            (q,) = _rope_proj(x, norm_mix_g[layer], sh1, sc1, diff_w_q[j].astype(BF16), positions,
                              out_scale=b_comp ** -0.5 * LOG2E, half=rope_dim // 2)
            mix = _attention(q, k, vt, diff_lambda[j], diff_subln_g[j], heads=b_heads, lam_init=_lambda_init(layer + 1))
            w_o = diff_w_out[j]
        x = _out_mlp(x, mix, g1, sh2, sc2, g2, norm_mlp_g[layer], w_o.astype(BF16), mlp_w_up[layer].astype(BF16),
                     mlp_w_down[layer].astype(BF16), final_norm_g, final=(layer == depth - 1))
        if layer == n_a - 1:
            n_k = b_heads * 2 * b_comp
            k, vt = _rope_proj(x, kv_norm_g, kv_mods[:, :, :d], kv_mods[:, :, d:], kv_w[:, :n_k].astype(BF16), positions,
                               kv_w[:, n_k:].T.astype(BF16), out_scale=1.0, half=rope_dim // 2)
    return x
```

```python
import functools
import math

import jax
import jax.numpy as jnp
from jax import lax
from jax.experimental import pallas as pl
from jax.experimental.pallas import tpu as pltpu

F32 = jnp.float32
BF16 = jnp.bfloat16

NORM_EPS = 1e-6
SUBLN_EPS = 1e-5
ROPE_THETA = 500000.0
PROJ_TOKENS = 1024
ATTN_AHEAD = 2
MLSTM_CHUNK = 256

V7X_VMEM_LIMIT_BYTES = 58 * 1024 * 1024
LANES = 128
BF16_SUBLANES = 16
LOG2E = math.log2(math.e)

_NT = (((1,), (1,)), ((), ()))
_TN = (((0,), (0,)), ((), ()))


def _tile(n, pref):
    t = min(n, pref)
    assert n % t == 0, (n, t)
    return t


def _params(sem):
    return pltpu.CompilerParams(dimension_semantics=sem, vmem_limit_bytes=V7X_VMEM_LIMIT_BYTES)


def _resident(shape):
    nd = len(shape)
    return pl.BlockSpec(shape, lambda *_: (0,) * nd, pipeline_mode=pl.Buffered(1))


def _rms(x, eps):
    return x * lax.rsqrt(jnp.mean(x * x, axis=-1, keepdims=True) + eps)


def _norm_mod(x, g, shift, scale):
    return (_rms(x, NORM_EPS) * g) * (1.0 + scale) + shift


def _mods_kernel(c_ref, w_ref, b_ref, o_ref):
    c = c_ref[...]
    act = (c * jax.nn.sigmoid(c)).astype(BF16)
    o_ref[0] = jnp.dot(act, w_ref[0].astype(BF16), preferred_element_type=F32) + b_ref[0]


def _mods(c, w, b):
    nl, d, n = w.shape
    bsz = c.shape[0]
    tn = _tile(n, 1024)
    return pl.pallas_call(
        _mods_kernel,
        grid=(nl, n // tn),
        in_specs=[
            pl.BlockSpec((bsz, d), lambda l, j: (0, 0)),
            pl.BlockSpec((1, d, tn), lambda l, j: (l, 0, j)),
            pl.BlockSpec((1, 1, tn), lambda l, j: (l, 0, j)),
        ],
        out_specs=pl.BlockSpec((1, bsz, tn), lambda l, j: (l, 0, j)),
        out_shape=jax.ShapeDtypeStruct((nl, bsz, n), F32),
        compiler_params=_params(("arbitrary", "arbitrary")),
        name="mods",
    )(c, w, b.reshape(nl, 1, n))


def _rope_tables(pos_row, half):
    comp_dim = LANES // 2
    tm = pos_row.shape[1]
    fidx = lax.broadcasted_iota(jnp.int32, (half, 1), 0).astype(F32)
    inv_freq = jnp.exp(fidx * (-math.log(ROPE_THETA) / half))
    ang = inv_freq * pos_row.astype(F32)
    rows = 4 * half
    r_idx = lax.broadcasted_iota(jnp.int32, (rows, LANES), 0)
    lane = lax.broadcasted_iota(jnp.int32, (rows, LANES), 1) % comp_dim
    hit = (r_idx < 3 * half) & (r_idx % half == lane % half)
    sel_cos = jnp.where(hit & (lane < 2 * half), 1.0, 0.0).astype(BF16)
    sel_sin = jnp.where(hit & (lane < half), -1.0, jnp.where(hit & (lane < 2 * half), 1.0, 0.0)).astype(BF16)
    lane1 = lax.broadcasted_iota(jnp.int32, (1, LANES), 1) % comp_dim
    ones_elsewhere = jnp.where(lane1 < 2 * half, 0.0, 1.0)

    def expand(v, sel):
        stacked = jnp.concatenate(list(_split3(v)) + [jnp.zeros((half, tm), BF16)], axis=0)
        return lax.dot_general(stacked, sel, _TN, preferred_element_type=F32)

    return expand(jnp.cos(ang), sel_cos) + ones_elsewhere, expand(jnp.sin(ang), sel_sin)


def _apply_rope(r, cos_t, sin_t, half):
    lane = lax.broadcasted_iota(jnp.int32, cos_t.shape, 1)
    pos = lane % (LANES // 2)
    src = jnp.where(pos < half, lane + half, jnp.where(pos < 2 * half, lane - half, lane))
    outs = []
    for j in range(r.shape[1] // LANES):
        blk = r[:, j * LANES:(j + 1) * LANES]
        partner = jnp.take_along_axis(blk, src, axis=1)
        outs.append(blk * cos_t + partner * sin_t)
    return outs


def _rope_proj_kernel(x_ref, g_ref, sh_ref, sc_ref, w_ref, pos_ref, *rest, half, out_scale, nc, with_vt):
    y = _norm_mod(x_ref[0], g_ref[...], sh_ref[0], sc_ref[0]).astype(BF16)
    cos_t, sin_t = _rope_tables(pos_ref[0, 0], half)
    o_ref = rest[-2] if with_vt else rest[-1]
    for c in range(o_ref.shape[-1] // nc):
        r = jnp.dot(y, w_ref[:, c * nc:(c + 1) * nc], preferred_element_type=F32)
        for j, blk in enumerate(_apply_rope(r, cos_t, sin_t, half)):
            if out_scale != 1.0:
                blk = blk * out_scale
            o_ref[0, :, c * nc + j * LANES:c * nc + (j + 1) * LANES] = blk.astype(o_ref.dtype)
    if with_vt:
        wvt_ref, vt_ref = rest[0], rest[-1]
        for c in range(vt_ref.shape[1] // nc):
            r = lax.dot_general(wvt_ref[c * nc:(c + 1) * nc, :], y, _NT, preferred_element_type=F32)
            vt_ref[0, c * nc:(c + 1) * nc, :] = r.astype(vt_ref.dtype)


def _rope_proj(x, g, shift, scale, w, positions, w_vt=None, *, out_scale, half):
    bsz, s, d = x.shape
    n = w.shape[1]
    tm = _tile(s, PROJ_TOKENS)
    nt = s // tm
    with_vt = w_vt is not None
    kern = functools.partial(_rope_proj_kernel, half=half, out_scale=out_scale, nc=_tile(n, 512), with_vt=with_vt)
    vec = pl.BlockSpec((1, 1, d), lambda b, i: (b, 0, 0))
    pos = pl.BlockSpec((1, 1, 1, tm), lambda b, i: (b, i, 0, 0))
    in_specs = [pl.BlockSpec((1, tm, d), lambda b, i: (b, i, 0)), _resident((1, d)), vec, vec, _resident((d, n)), pos]
    out_specs = [pl.BlockSpec((1, tm, n), lambda b, i: (b, i, 0))]
    out_shape = [jax.ShapeDtypeStruct((bsz, s, n), BF16)]
    args = [x, g.reshape(1, d), shift, scale, w, positions.reshape(bsz, nt, 1, tm)]
    if with_vt:
        nv = w_vt.shape[0]
        in_specs.append(_resident((nv, d)))
        out_specs.append(pl.BlockSpec((1, nv, tm), lambda b, i: (b, 0, i)))
        out_shape.append(jax.ShapeDtypeStruct((bsz, nv, s), BF16))
        args.append(w_vt)
    return pl.pallas_call(
        kern,
        grid=(bsz, nt),
        in_specs=in_specs,
        out_specs=out_specs,
        out_shape=out_shape,
        compiler_params=_params(("arbitrary", "arbitrary")),
        name="rope_proj",
    )(*args)


def _mlstm_proj_kernel(x_ref, g_ref, sh_ref, sc_ref, w_ref, wkt_ref, wg_ref, bg_ref, hg_ref, voq_ref, kt_ref, gates_ref, *, v_cols, k_scale, heads, nc, chunk):
    y = _norm_mod(x_ref[0], g_ref[...], sh_ref[0], sc_ref[0]).astype(BF16)
    for c in range(w_ref.shape[1] // nc):
        col = c * nc
        r = jnp.dot(y, w_ref[:, col:col + nc], preferred_element_type=F32)
        if v_cols <= col < 2 * v_cols:
            r = jax.nn.sigmoid(r) * hg_ref[:, col - v_cols:col - v_cols + nc]
        voq_ref[0, :, col:col + nc] = r.astype(BF16)
    kt = lax.dot_general(wkt_ref[...], y, _NT, preferred_element_type=F32) * k_scale
    kt_ref[0] = kt.astype(BF16)
    gt = lax.dot_general(wg_ref[...], y, _NT, preferred_element_type=F32) + bg_ref[...]
    row = lax.broadcasted_iota(jnp.int32, gt.shape, 0)
    logf = jnp.minimum(gt, 0.0) - jnp.log1p(jnp.exp(-jnp.abs(gt)))
    gt = jnp.where(row < heads, gt, logf)
    for c in range(gt.shape[1] // chunk):
        gates_ref[0, c] = gt[:, c * chunk:(c + 1) * chunk]


def _mlstm_proj(x, g, shift, scale, w_voq, w_kt, w_gates_t, b_gates, head_g, *, v_cols, k_scale, heads, chunk):
    bsz, s, d = x.shape
    n = w_voq.shape[1]
    q_cols = w_kt.shape[0]
    tm = _tile(s, PROJ_TOKENS)
    kern = functools.partial(_mlstm_proj_kernel, v_cols=v_cols, k_scale=k_scale, heads=heads,
                             nc=_tile(q_cols, 512), chunk=chunk)
    vec = pl.BlockSpec((1, 1, d), lambda b, i: (b, 0, 0))
    return pl.pallas_call(
        kern,
        grid=(bsz, s // tm),
        in_specs=[
            pl.BlockSpec((1, tm, d), lambda b, i: (b, i, 0)),
            _resident((1, d)), vec, vec,
            _resident((d, n)),
            _resident((q_cols, d)),
            _resident((2 * heads, d)),
            _resident((2 * heads, 1)),
            _resident((1, v_cols)),
        ],
        out_specs=[
            pl.BlockSpec((1, tm, n), lambda b, i: (b, i, 0)),
            pl.BlockSpec((1, q_cols, tm), lambda b, i: (b, 0, i)),
            pl.BlockSpec((1, tm // chunk, 2 * heads, chunk), lambda b, i: (b, i, 0, 0)),
        ],
        out_shape=[
            jax.ShapeDtypeStruct((bsz, s, n), BF16),
            jax.ShapeDtypeStruct((bsz, q_cols, s), BF16),
            jax.ShapeDtypeStruct((bsz, s // chunk, 2 * heads, chunk), F32),
        ],
        compiler_params=_params(("arbitrary", "arbitrary")),
        name="mlstm_proj",
    )(x, g.reshape(1, d), shift, scale, w_voq, w_kt, w_gates_t, b_gates, head_g.reshape(1, v_cols))


def _prefix_max(x):
    n = x.shape[1]
    lane = lax.broadcasted_iota(jnp.int32, x.shape, 1)
    shift = 1
    while shift < n:
        x = jnp.maximum(x, jnp.where(lane >= shift, pltpu.roll(x, shift, axis=1), -jnp.inf))
        shift *= 2
    return x


def _split3(x):
    hi = x.astype(BF16)
    r1 = x - hi.astype(F32)
    mid = r1.astype(BF16)
    lo = (r1 - mid.astype(F32)).astype(BF16)
    return hi, mid, lo


def _mlstm_kernel(v_ref, og_ref, q_ref, kt_ref, gates_ref, gates_nb_ref, out_ref, c_sc, n_sc, m_sc, rows_sc, cols_sc, *, heads, dk, dv):
    H = heads
    L = q_ref.shape[1]
    step_i = pl.program_id(1)

    row = lax.broadcasted_iota(jnp.int32, (L, L), 0)
    col = lax.broadcasted_iota(jnp.int32, (L, L), 1)
    causal = col <= row
    upper = (row <= col).astype(F32)

    def gate_rows(gr):
        g = jnp.dot(gr, upper, precision=lax.Precision.HIGHEST, preferred_element_type=F32)[H:]
        return gr[:H] - g, g

    def gate_cols(b, g):
        rows = jnp.concatenate([_prefix_max(b), g], axis=0)
        stacked = jnp.concatenate(list(_split3(rows)) + [jnp.zeros_like(rows, dtype=BF16)], axis=0)
        sel_r = lax.broadcasted_iota(jnp.int32, (8 * H, 2 * H * LANES), 0) % (2 * H)
        sel_c = lax.broadcasted_iota(jnp.int32, (8 * H, 2 * H * LANES), 1) // LANES
        return lax.dot_general(stacked, (sel_r == sel_c).astype(BF16), _TN, preferred_element_type=F32)

    @pl.when(step_i == 0)
    def _():
        c_sc[...] = jnp.zeros_like(c_sc)
        n_sc[...] = jnp.zeros_like(n_sc)
        m_sc[...] = jnp.zeros_like(m_sc)

    @pl.when((step_i == 0) & (pl.program_id(0) == 0))
    def _():
        b0, g0 = gate_rows(gates_ref[0, 0])
        rows_sc[...] = jnp.concatenate([b0, g0], axis=0)
        cols_sc[...] = gate_cols(b0, g0)

    ones = jnp.ones((L, LANES), BF16)
    ones_v = jnp.ones((dv, LANES), BF16)
    rep_l = L // LANES
    rep_v = dv // LANES
    rep_k = dk // LANES

    hs = range(H)
    m_old = [m_sc[h] for h in hs]
    c_old = [c_sc[h] for h in hs]
    n_old = [n_sc[h] for h in hs]
    qh = [q_ref[0, :, h * dk:(h + 1) * dk] for h in hs]
    kth = [kt_ref[0, h * dk:(h + 1) * dk, :] for h in hs]
    vh = [v_ref[0, :, h * dv:(h + 1) * dv] for h in hs]
    rows = rows_sc[...]
    b, g = rows[:H], rows[H:]
    cols = cols_sc[...]
    last = step_i == pl.num_programs(1) - 1
    gr_next = jnp.where(last, gates_nb_ref[0, 0], gates_ref[0, jnp.minimum(step_i + 1, pl.num_programs(1) - 1)])
    b_next, g_next = gate_rows(gr_next)

    s = [jnp.dot(qh[h], kth[h], preferred_element_type=F32) for h in hs]
    b_row = [b[h:h + 1] for h in hs]
    m_end = [jnp.maximum(m_old[h], jnp.max(b_row[h], axis=-1, keepdims=True)) for h in hs]
    c_new, n_new = [], []
    for h in hs:
        w_old = jnp.exp(m_old[h] - m_end[h])
        kw = (kth[h].astype(F32) * jnp.exp(b_row[h] - m_end[h])).astype(BF16)
        c_new.append(w_old * c_old[h] + jnp.dot(kw, vh[h], preferred_element_type=F32))
        n_new.append(w_old * n_old[h] + jnp.dot(kw, ones, preferred_element_type=F32))

    m_c = [jnp.maximum(cols[:, h * LANES:(h + 1) * LANES], m_old[h]) for h in hs]

    num, den = [], []
    for h in hs:
        w = jnp.exp(jnp.where(causal, b_row[h] - jnp.tile(m_c[h], (1, rep_l)), -jnp.inf))
        p = (s[h] * w).astype(BF16)
        w_inter = jnp.exp(m_old[h] - m_c[h])
        qw = (qh[h].astype(F32) * jnp.tile(w_inter, (1, rep_k))).astype(BF16)
        lhs = jnp.concatenate([p, qw], axis=1)
        num.append(jnp.dot(lhs, jnp.concatenate([vh[h], c_old[h].astype(BF16)], axis=0), preferred_element_type=F32))
        den.append(jnp.dot(lhs, jnp.concatenate([ones, n_old[h].astype(BF16)], axis=0), preferred_element_type=F32))

    ssq = [jnp.dot((num[h] * num[h]).astype(BF16), ones_v, preferred_element_type=F32) for h in hs]
    for h in hs:
        g_c = cols[:, (H + h) * LANES:(H + h + 1) * LANES]
        inv = 1.0 / jnp.maximum(jnp.abs(den[h]), jnp.exp(-(g_c + m_c[h])))
        scale = inv * lax.rsqrt(inv * inv * ssq[h] * (1.0 / dv) + NORM_EPS)
        out = num[h] * jnp.tile(scale, (1, rep_v)) * og_ref[0, :, h * dv:(h + 1) * dv].astype(F32)
        out_ref[0, :, h * dv:(h + 1) * dv] = out.astype(out_ref.dtype)

    for h in hs:
        c_sc[h] = c_new[h]
        n_sc[h] = n_new[h]
        m_sc[h] = g[h:h + 1, L - 1:L] + m_end[h]
    cols_sc[...] = gate_cols(b_next, g_next)
    rows_sc[...] = jnp.concatenate([b_next, g_next], axis=0)


def _mlstm(voq, kt, gates, *, heads, dk, dv, chunk):
    bsz, s, _ = voq.shape
    qc, vc = heads * dk, heads * dv
    kern = functools.partial(_mlstm_kernel, heads=heads, dk=dk, dv=dv)
    return pl.pallas_call(
        kern,
        grid=(bsz, s // chunk),
        in_specs=[
            pl.BlockSpec((1, chunk, vc), lambda b, i: (b, i, 0)),
            pl.BlockSpec((1, chunk, vc), lambda b, i: (b, i, 1)),
            pl.BlockSpec((1, chunk, qc), lambda b, i: (b, i, (2 * vc) // qc)),
            pl.BlockSpec((1, qc, chunk), lambda b, i: (b, 0, i)),
            pl.BlockSpec((1, s // chunk, 2 * heads, chunk), lambda b, i: (b, 0, 0, 0)),
            pl.BlockSpec((1, 1, 2 * heads, chunk), lambda b, i: (jnp.minimum(b + 1, bsz - 1), 0, 0, 0)),
        ],
        out_specs=pl.BlockSpec((1, chunk, vc), lambda b, i: (b, i, 0)),
        out_shape=jax.ShapeDtypeStruct((bsz, s, vc), BF16),
        scratch_shapes=[
            pltpu.VMEM((heads, dk, dv), F32),
            pltpu.VMEM((heads, dk, LANES), F32),
            pltpu.VMEM((heads, 1, 1), F32),
            pltpu.VMEM((2 * heads, chunk), F32),
            pltpu.VMEM((chunk, 2 * heads * LANES), F32),
        ],
        compiler_params=_params(("arbitrary", "arbitrary")),
        name="mlstm",
    )(voq, voq, voq, kt, gates, gates)


def _attn_kernel(q_ref, k_ref, vt_ref, lam_ref, g_ref, cap_ref, o_ref, m_sc, acc_sc, s_sc, *, tq, tk, hd, nh, ahead, lam_init):
    lp = lam_ref[...]
    lam = (jnp.exp(jnp.sum(lp[0:1] * lp[1:2], axis=-1, keepdims=True))
           - jnp.exp(jnp.sum(lp[2:3] * lp[3:4], axis=-1, keepdims=True)) + lam_init)
    g_col = g_ref[...] * (1.0 - lam_init)
    comp_dim = hd // 2
    assert tq == tk and ahead <= nh

    def q_tile(qi, carry):
        q0 = pl.multiple_of(qi * tq, tq)
        qqs = []
        for h in range(nh):
            q = q_ref[0, pl.ds(q0, tq), h * hd:(h + 1) * hd]
            lane = lax.broadcasted_iota(jnp.int32, q.shape, 1)
            zero = jnp.zeros_like(q)
            qqs.append(jnp.concatenate([jnp.where(lane < comp_dim, q, zero), jnp.where(lane >= comp_dim, q, zero)], axis=0))
        m_sc[...] = jnp.full_like(m_sc, -jnp.inf)
        acc_sc[...] = jnp.zeros_like(acc_sc)
        ones = jnp.ones((acc_sc.shape[1] - hd, tk), BF16)

        def scores(k0, h):
            kk = k_ref[0, pl.ds(k0, tk), h * hd:(h + 1) * hd]
            return lax.dot_general(kk, qqs[h], _NT, preferred_element_type=F32)

        def step(j, c, diagonal):
            k0 = pl.multiple_of(j * tk, tk)
            pending = [s_sc[i] for i in range(ahead)]
            for h in range(nh):
                s = pending.pop(0)
                if diagonal:
                    s = jnp.minimum(s, cap_ref[...])
                if h + ahead < nh:
                    pending.append(scores(k0, h + ahead))
                elif not diagonal:
                    s_sc[h + ahead - nh] = scores(k0 + tk, h + ahead - nh)
                vt = jnp.concatenate([vt_ref[0, h * hd:(h + 1) * hd, pl.ds(k0, tk)], ones], axis=0)
                m_old = m_sc[h]
                m_new = jnp.maximum(m_old, jnp.max(s, axis=0, keepdims=True))
                alpha = jnp.exp2(m_old - m_new)
                p = jnp.exp2(s - m_new).astype(BF16)
                acc_sc[h] = alpha * acc_sc[h] + jnp.dot(vt, p, preferred_element_type=F32)
                m_sc[h] = m_new
            return c

        def two_steps(jj, c):
            step(2 * jj, c, False)
            return step(2 * jj + 1, c, False)

        for i in range(ahead):
            s_sc[i] = scores(0, i)
        lax.fori_loop(0, qi // 2, two_steps, 0)
        lax.fori_loop(2 * (qi // 2), qi, functools.partial(step, diagonal=False), 0)
        lax.fori_loop(qi, qi + 1, functools.partial(step, diagonal=True), 0)

        for h in range(nh):
            acc = acc_sc[h]
            acc = acc[:hd] / acc[hd:hd + 1]
            o = acc[:, :tq] - lam * acc[:, tq:]
            o = o * lax.rsqrt(jnp.mean(o * o, axis=0, keepdims=True) + SUBLN_EPS) * g_col
            o_ref[0, pl.ds(q0, tq), h * hd:(h + 1) * hd] = o.T.astype(o_ref.dtype)
        return carry

    lax.fori_loop(0, q_ref.shape[1] // tq, q_tile, 0)


def _attention(q, k, vt, lam_params, subln_g, *, heads, lam_init):
    bsz, s, d = q.shape
    hd = d // heads
    tq = _tile(s, 256)
    tk = tq
    nh = _tile(heads, 8)
    kern = functools.partial(_attn_kernel, tq=tq, tk=tk, hd=hd, nh=nh, ahead=ATTN_AHEAD, lam_init=lam_init)
    kpos = lax.broadcasted_iota(jnp.int32, (tk, 2 * tq), 0)
    qpos = lax.broadcasted_iota(jnp.int32, (tk, 2 * tq), 1) % tq
    cap = jnp.where(kpos <= qpos, jnp.inf, -jnp.inf).astype(F32)
    return pl.pallas_call(
        kern,
        grid=(bsz, heads // nh),
        in_specs=[
            pl.BlockSpec((1, s, nh * hd), lambda b, h: (b, 0, h)),
            pl.BlockSpec((1, s, nh * hd), lambda b, h: (b, 0, h)),
            pl.BlockSpec((1, nh * hd, s), lambda b, h: (b, h, 0)),
            _resident(lam_params.shape),
            _resident((hd, 1)),
            _resident(cap.shape),
        ],
        out_specs=pl.BlockSpec((1, s, nh * hd), lambda b, h: (b, 0, h)),
        out_shape=jax.ShapeDtypeStruct((bsz, s, d), BF16),
        scratch_shapes=[
            pltpu.VMEM((nh, 1, 2 * tq), F32),
            pltpu.VMEM((nh, hd + BF16_SUBLANES, 2 * tq), F32),
            pltpu.VMEM((ATTN_AHEAD, tk, 2 * tq), F32),
        ],
        compiler_params=_params(("arbitrary", "arbitrary")),
        name="diff_attn",
    )(q, k, vt, lam_params, subln_g.reshape(hd, 1), cap)


def _out_mlp_kernel(x_ref, a_ref, g1_ref, sh_ref, sc_ref, g2_ref, ng_ref, wo_ref, wu_ref, wd_ref, fg_ref, o_ref, *, fc, final):
    x1 = x_ref[0] + g1_ref[0] * jnp.dot(a_ref[0], wo_ref[...], preferred_element_type=F32)
    y = _norm_mod(x1, ng_ref[...], sh_ref[0], sc_ref[0]).astype(BF16)
    acc = jnp.zeros_like(x1)
    for c in range(wu_ref.shape[1] // fc):
        h = jnp.dot(y, wu_ref[:, c * fc:(c + 1) * fc], preferred_element_type=F32)
        h = jnp.square(jnp.maximum(h, 0.0)).astype(BF16)
        acc = acc + jnp.dot(h, wd_ref[c * fc:(c + 1) * fc, :], preferred_element_type=F32)
    x2 = x1 + g2_ref[0] * acc
    if final:
        x2 = _rms(x2, NORM_EPS) * fg_ref[...]
    o_ref[0] = x2


def _out_mlp(x, a, gate1, shift2, scale2, gate2, norm_g, w_out, w_up, w_down, final_g, *, final):
    bsz, s, d = x.shape
    da = a.shape[-1]
    f = w_up.shape[1]
    tm = _tile(s, 1024)
    kern = functools.partial(_out_mlp_kernel, fc=_tile(f, 512), final=final)
    vec = pl.BlockSpec((1, 1, d), lambda b, i: (b, 0, 0))
    return pl.pallas_call(
        kern,
        grid=(bsz, s // tm),
        in_specs=[
            pl.BlockSpec((1, tm, d), lambda b, i: (b, i, 0)),
            pl.BlockSpec((1, tm, da), lambda b, i: (b, i, 0)),
            vec, vec, vec, vec,
            _resident((1, d)),
            _resident((da, d)),
            _resident((d, f)),
            _resident((f, d)),
            _resident((1, d)),
        ],
        out_specs=pl.BlockSpec((1, tm, d), lambda b, i: (b, i, 0)),
        out_shape=jax.ShapeDtypeStruct((bsz, s, d), F32),
        compiler_params=_params(("arbitrary", "arbitrary")),
        name="out_mlp",
    )(x, a, gate1, shift2, scale2, gate2, norm_g.reshape(1, d), w_out, w_up, w_down, final_g.reshape(1, d))


def _lambda_init(layer_number):
    return 0.8 - 0.6 * math.exp(-0.3 * (layer_number - 1))


def kernel(x, c, positions, ada_w, ada_b, norm_mix_g, norm_mlp_g, mlp_w_up, mlp_w_down, mlstm_w_in, mlstm_b_gates, mlstm_head_g, mlstm_w_out, kv_norm_g, kv_ada_w, kv_ada_b, kv_w, diff_w_q, diff_lambda, diff_subln_g, diff_w_out, final_norm_g):
    bsz, s, d = x.shape
    depth = ada_w.shape[0]
    n_a = mlstm_w_in.shape[0]
    a_heads = mlstm_b_gates.shape[-1]
    a_v_cols = mlstm_w_out.shape[1]
    a_q_cols = (mlstm_w_in.shape[-1] - 2 * a_v_cols - 2 * a_heads) // 2
    a_dk, a_dv = a_q_cols // a_heads, a_v_cols // a_heads
    b_comp = diff_lambda.shape[-1]
    b_heads = diff_w_q.shape[-1] // (2 * b_comp)
    rope_dim = b_comp // 4

    mods = _mods(c, ada_w, ada_b).reshape(depth, bsz, 1, 6 * d)
    kv_mods = _mods(c, kv_ada_w[None], kv_ada_b[None]).reshape(bsz, 1, 2 * d)

    k = vt = None
    for layer in range(depth):
        m = mods[layer]
        sh1, sc1, g1, sh2, sc2, g2 = [m[:, :, j * d:(j + 1) * d] for j in range(6)]
        if layer < n_a:
            w_in = mlstm_w_in[layer]
            n_main = 2 * a_q_cols + 2 * a_v_cols
            w_voq = jnp.concatenate([w_in[:, 2 * a_q_cols:n_main], w_in[:, :a_q_cols]], axis=1).astype(BF16)
            w_kt = w_in[:, a_q_cols:2 * a_q_cols].T.astype(BF16)
            w_gates_t = w_in[:, n_main:].T.astype(BF16)
            b_g = mlstm_b_gates[layer].reshape(2 * a_heads, 1)
            chunk = _tile(s, MLSTM_CHUNK)
            voq, kt, gates = _mlstm_proj(x, norm_mix_g[layer], sh1, sc1, w_voq, w_kt, w_gates_t, b_g, mlstm_head_g[layer],
                                         v_cols=a_v_cols, k_scale=a_dk ** -0.5, heads=a_heads, chunk=chunk)
            mix = _mlstm(voq, kt, gates, heads=a_heads, dk=a_dk, dv=a_dv, chunk=chunk)
            w_o = mlstm_w_out[layer]
        else:
            j = layer - n_a
            (q,) = _rope_proj(x, norm_mix_g[layer], sh1, sc1, diff_w_q[j].astype(BF16), positions,
                              out_scale=b_comp ** -0.5 * LOG2E, half=rope_dim // 2)
            mix = _attention(q, k, vt, diff_lambda[j], diff_subln_g[j], heads=b_heads, lam_init=_lambda_init(layer + 1))
            w_o = diff_w_out[j]
        x = _out_mlp(x, mix, g1, sh2, sc2, g2, norm_mlp_g[layer], w_o.astype(BF16), mlp_w_up[layer].astype(BF16),
                     mlp_w_down[layer].astype(BF16), final_norm_g, final=(layer == depth - 1))
        if layer == n_a - 1:
            n_k = b_heads * 2 * b_comp
            k, vt = _rope_proj(x, kv_norm_g, kv_mods[:, :, :d], kv_mods[:, :, d:], kv_w[:, :n_k].astype(BF16), positions,
                               kv_w[:, n_k:].T.astype(BF16), out_scale=1.0, half=rope_dim // 2)
    return x
```

```python
import functools
import math

import jax
import jax.numpy as jnp
from jax import lax
from jax.experimental import pallas as pl
from jax.experimental.pallas import tpu as pltpu

F32 = jnp.float32
BF16 = jnp.bfloat16

NORM_EPS = 1e-6
SUBLN_EPS = 1e-5
ROPE_THETA = 500000.0
PROJ_TOKENS = 1024
ATTN_AHEAD = 2
MLSTM_CHUNK = 256

V7X_VMEM_LIMIT_BYTES = 58 * 1024 * 1024
LANES = 128
BF16_SUBLANES = 16
LOG2E = math.log2(math.e)

_NT = (((1,), (1,)), ((), ()))
_TN = (((0,), (0,)), ((), ()))


def _tile(n, pref):
    t = min(n, pref)
    assert n % t == 0, (n, t)
    return t


def _params(sem):
    return pltpu.CompilerParams(dimension_semantics=sem, vmem_limit_bytes=V7X_VMEM_LIMIT_BYTES)


def _resident(shape):
    nd = len(shape)
    return pl.BlockSpec(shape, lambda *_: (0,) * nd, pipeline_mode=pl.Buffered(1))


def _rms(x, eps):
    return x * lax.rsqrt(jnp.mean(x * x, axis=-1, keepdims=True) + eps)


def _norm_mod(x, g, shift, scale):
    return (_rms(x, NORM_EPS) * g) * (1.0 + scale) + shift


def _mods_kernel(c_ref, w_ref, b_ref, o_ref):
    c = c_ref[...]
    act = (c * jax.nn.sigmoid(c)).astype(BF16)
    o_ref[0] = jnp.dot(act, w_ref[0].astype(BF16), preferred_element_type=F32) + b_ref[0]


def _mods(c, w, b):
    nl, d, n = w.shape
    bsz = c.shape[0]
    tn = _tile(n, 1024)
    return pl.pallas_call(
        _mods_kernel,
        grid=(nl, n // tn),
        in_specs=[
            pl.BlockSpec((bsz, d), lambda l, j: (0, 0)),
            pl.BlockSpec((1, d, tn), lambda l, j: (l, 0, j)),
            pl.BlockSpec((1, 1, tn), lambda l, j: (l, 0, j)),
        ],
        out_specs=pl.BlockSpec((1, bsz, tn), lambda l, j: (l, 0, j)),
        out_shape=jax.ShapeDtypeStruct((nl, bsz, n), F32),
        compiler_params=_params(("arbitrary", "arbitrary")),
        name="mods",
    )(c, w, b.reshape(nl, 1, n))


def _rope_tables(pos_row, half):
    comp_dim = LANES // 2
    tm = pos_row.shape[1]
    fidx = lax.broadcasted_iota(jnp.int32, (half, 1), 0).astype(F32)
    inv_freq = jnp.exp(fidx * (-math.log(ROPE_THETA) / half))
    ang = inv_freq * pos_row.astype(F32)
    rows = 4 * half
    r_idx = lax.broadcasted_iota(jnp.int32, (rows, LANES), 0)
    lane = lax.broadcasted_iota(jnp.int32, (rows, LANES), 1) % comp_dim
    hit = (r_idx < 3 * half) & (r_idx % half == lane % half)
    sel_cos = jnp.where(hit & (lane < 2 * half), 1.0, 0.0).astype(BF16)
    sel_sin = jnp.where(hit & (lane < half), -1.0, jnp.where(hit & (lane < 2 * half), 1.0, 0.0)).astype(BF16)
    lane1 = lax.broadcasted_iota(jnp.int32, (1, LANES), 1) % comp_dim
    ones_elsewhere = jnp.where(lane1 < 2 * half, 0.0, 1.0)

    def expand(v, sel):
        stacked = jnp.concatenate(list(_split3(v)) + [jnp.zeros((half, tm), BF16)], axis=0)
        return lax.dot_general(stacked, sel, _TN, preferred_element_type=F32)

    return expand(jnp.cos(ang), sel_cos) + ones_elsewhere, expand(jnp.sin(ang), sel_sin)


def _apply_rope(r, cos_t, sin_t, half):
    lane = lax.broadcasted_iota(jnp.int32, cos_t.shape, 1)
    pos = lane % (LANES // 2)
    src = jnp.where(pos < half, lane + half, jnp.where(pos < 2 * half, lane - half, lane))
    outs = []
    for j in range(r.shape[1] // LANES):
        blk = r[:, j * LANES:(j + 1) * LANES]
        partner = jnp.take_along_axis(blk, src, axis=1)
        outs.append(blk * cos_t + partner * sin_t)
    return outs


def _rope_proj_kernel(x_ref, g_ref, sh_ref, sc_ref, w_ref, pos_ref, *rest, half, out_scale, nc, with_vt):
    y = _norm_mod(x_ref[0], g_ref[...], sh_ref[0], sc_ref[0]).astype(BF16)
    cos_t, sin_t = _rope_tables(pos_ref[0, 0], half)
    o_ref = rest[-2] if with_vt else rest[-1]
    for c in range(o_ref.shape[-1] // nc):
        r = jnp.dot(y, w_ref[:, c * nc:(c + 1) * nc], preferred_element_type=F32)
        for j, blk in enumerate(_apply_rope(r, cos_t, sin_t, half)):
            if out_scale != 1.0:
                blk = blk * out_scale
            o_ref[0, :, c * nc + j * LANES:c * nc + (j + 1) * LANES] = blk.astype(o_ref.dtype)
    if with_vt:
        wvt_ref, vt_ref = rest[0], rest[-1]
        for c in range(vt_ref.shape[1] // nc):
            r = lax.dot_general(wvt_ref[c * nc:(c + 1) * nc, :], y, _NT, preferred_element_type=F32)
            vt_ref[0, c * nc:(c + 1) * nc, :] = r.astype(vt_ref.dtype)


def _rope_proj(x, g, shift, scale, w, positions, w_vt=None, *, out_scale, half):
    bsz, s, d = x.shape
    n = w.shape[1]
    tm = _tile(s, PROJ_TOKENS)
    nt = s // tm
    with_vt = w_vt is not None
    kern = functools.partial(_rope_proj_kernel, half=half, out_scale=out_scale, nc=_tile(n, 512), with_vt=with_vt)
    vec = pl.BlockSpec((1, 1, d), lambda b, i: (b, 0, 0))
    pos = pl.BlockSpec((1, 1, 1, tm), lambda b, i: (b, i, 0, 0))
    in_specs = [pl.BlockSpec((1, tm, d), lambda b, i: (b, i, 0)), _resident((1, d)), vec, vec, _resident((d, n)), pos]
    out_specs = [pl.BlockSpec((1, tm, n), lambda b, i: (b, i, 0))]
    out_shape = [jax.ShapeDtypeStruct((bsz, s, n), BF16)]
    args = [x, g.reshape(1, d), shift, scale, w, positions.reshape(bsz, nt, 1, tm)]
    if with_vt:
        nv = w_vt.shape[0]
        in_specs.append(_resident((nv, d)))
        out_specs.append(pl.BlockSpec((1, nv, tm), lambda b, i: (b, 0, i)))
        out_shape.append(jax.ShapeDtypeStruct((bsz, nv, s), BF16))
        args.append(w_vt)
    return pl.pallas_call(
        kern,
        grid=(bsz, nt),
        in_specs=in_specs,
        out_specs=out_specs,
        out_shape=out_shape,
        compiler_params=_params(("arbitrary", "arbitrary")),
        name="rope_proj",
    )(*args)


def _mlstm_proj_kernel(x_ref, g_ref, sh_ref, sc_ref, w_ref, wkt_ref, wg_ref, bg_ref, hg_ref, voq_ref, kt_ref, gates_ref, *, v_cols, k_scale, heads, nc, chunk):
    y = _norm_mod(x_ref[0], g_ref[...], sh_ref[0], sc_ref[0]).astype(BF16)
    for c in range(w_ref.shape[1] // nc):
        col = c * nc
        r = jnp.dot(y, w_ref[:, col:col + nc], preferred_element_type=F32)
        if v_cols <= col < 2 * v_cols:
            r = jax.nn.sigmoid(r) * hg_ref[:, col - v_cols:col - v_cols + nc]
        voq_ref[0, :, col:col + nc] = r.astype(BF16)
    kt = lax.dot_general(wkt_ref[...], y, _NT, preferred_element_type=F32) * k_scale
    kt_ref[0] = kt.astype(BF16)
    gt = lax.dot_general(wg_ref[...], y, _NT, preferred_element_type=F32) + bg_ref[...]
    row = lax.broadcasted_iota(jnp.int32, gt.shape, 0)
    logf = jnp.minimum(gt, 0.0) - jnp.log1p(jnp.exp(-jnp.abs(gt)))
    gt = jnp.where(row < heads, gt, logf)
    for c in range(gt.shape[1] // chunk):
        gates_ref[0, c] = gt[:, c * chunk:(c + 1) * chunk]


def _mlstm_proj(x, g, shift, scale, w_voq, w_kt, w_gates_t, b_gates, head_g, *, v_cols, k_scale, heads, chunk):
    bsz, s, d = x.shape
    n = w_voq.shape[1]
    q_cols = w_kt.shape[0]
    tm = _tile(s, PROJ_TOKENS)
    kern = functools.partial(_mlstm_proj_kernel, v_cols=v_cols, k_scale=k_scale, heads=heads,
                             nc=_tile(q_cols, 512), chunk=chunk)
    vec = pl.BlockSpec((1, 1, d), lambda b, i: (b, 0, 0))
    return pl.pallas_call(
        kern,
        grid=(bsz, s // tm),
        in_specs=[
            pl.BlockSpec((1, tm, d), lambda b, i: (b, i, 0)),
            _resident((1, d)), vec, vec,
            _resident((d, n)),
            _resident((q_cols, d)),
            _resident((2 * heads, d)),
            _resident((2 * heads, 1)),
            _resident((1, v_cols)),
        ],
        out_specs=[
            pl.BlockSpec((1, tm, n), lambda b, i: (b, i, 0)),
            pl.BlockSpec((1, q_cols, tm), lambda b, i: (b, 0, i)),
            pl.BlockSpec((1, tm // chunk, 2 * heads, chunk), lambda b, i: (b, i, 0, 0)),
        ],
        out_shape=[
            jax.ShapeDtypeStruct((bsz, s, n), BF16),
            jax.ShapeDtypeStruct((bsz, q_cols, s), BF16),
            jax.ShapeDtypeStruct((bsz, s // chunk, 2 * heads, chunk), F32),
        ],
        compiler_params=_params(("arbitrary", "arbitrary")),
        name="mlstm_proj",
    )(x, g.reshape(1, d), shift, scale, w_voq, w_kt, w_gates_t, b_gates, head_g.reshape(1, v_cols))


def _prefix_max(x):
    n = x.shape[1]
    lane = lax.broadcasted_iota(jnp.int32, x.shape, 1)
    shift = 1
    while shift < n:
        x = jnp.maximum(x, jnp.where(lane >= shift, pltpu.roll(x, shift, axis=1), -jnp.inf))
        shift *= 2
    return x


def _split3(x):
    hi = x.astype(BF16)
    r1 = x - hi.astype(F32)
    mid = r1.astype(BF16)
    lo = (r1 - mid.astype(F32)).astype(BF16)
    return hi, mid, lo


def _mlstm_kernel(v_ref, og_ref, q_ref, kt_ref, gates_ref, gates_nb_ref, out_ref, c_sc, n_sc, m_sc, rows_sc, cols_sc, *, heads, dk, dv):
    H = heads
    L = q_ref.shape[1]
    step_i = pl.program_id(1)

    row = lax.broadcasted_iota(jnp.int32, (L, L), 0)
    col = lax.broadcasted_iota(jnp.int32, (L, L), 1)
    causal = col <= row
    upper = (row <= col).astype(F32)

    def gate_rows(gr):
        g = jnp.dot(gr, upper, precision=lax.Precision.HIGHEST, preferred_element_type=F32)[H:]
        return gr[:H] - g, g

    def gate_cols(b, g):
        rows = jnp.concatenate([_prefix_max(b), g], axis=0)
        stacked = jnp.concatenate(list(_split3(rows)) + [jnp.zeros_like(rows, dtype=BF16)], axis=0)
        sel_r = lax.broadcasted_iota(jnp.int32, (8 * H, 2 * H * LANES), 0) % (2 * H)
        sel_c = lax.broadcasted_iota(jnp.int32, (8 * H, 2 * H * LANES), 1) // LANES
        return lax.dot_general(stacked, (sel_r == sel_c).astype(BF16), _TN, preferred_element_type=F32)

    @pl.when(step_i == 0)
    def _():
        c_sc[...] = jnp.zeros_like(c_sc)
        n_sc[...] = jnp.zeros_like(n_sc)
        m_sc[...] = jnp.zeros_like(m_sc)

    @pl.when((step_i == 0) & (pl.program_id(0) == 0))
    def _():
        b0, g0 = gate_rows(gates_ref[0, 0])
        rows_sc[...] = jnp.concatenate([b0, g0], axis=0)
        cols_sc[...] = gate_cols(b0, g0)

    ones = jnp.ones((L, LANES), BF16)
    ones_v = jnp.ones((dv, LANES), BF16)
    rep_l = L // LANES
    rep_v = dv // LANES
    rep_k = dk // LANES

    hs = range(H)
    m_old = [m_sc[h] for h in hs]
    c_old = [c_sc[h] for h in hs]
    n_old = [n_sc[h] for h in hs]
    qh = [q_ref[0, :, h * dk:(h + 1) * dk] for h in hs]
    kth = [kt_ref[0, h * dk:(h + 1) * dk, :] for h in hs]
    vh = [v_ref[0, :, h * dv:(h + 1) * dv] for h in hs]
    rows = rows_sc[...]
    b, g = rows[:H], rows[H:]
    cols = cols_sc[...]
    last = step_i == pl.num_programs(1) - 1
    gr_next = jnp.where(last, gates_nb_ref[0, 0], gates_ref[0, jnp.minimum(step_i + 1, pl.num_programs(1) - 1)])
    b_next, g_next = gate_rows(gr_next)

    s = [jnp.dot(qh[h], kth[h], preferred_element_type=F32) for h in hs]
    b_row = [b[h:h + 1] for h in hs]
    m_end = [jnp.maximum(m_old[h], jnp.max(b_row[h], axis=-1, keepdims=True)) for h in hs]
    c_new, n_new = [], []
    for h in hs:
        w_old = jnp.exp(m_old[h] - m_end[h])
        kw = (kth[h].astype(F32) * jnp.exp(b_row[h] - m_end[h])).astype(BF16)
        c_new.append(w_old * c_old[h] + jnp.dot(kw, vh[h], preferred_element_type=F32))
        n_new.append(w_old * n_old[h] + jnp.dot(kw, ones, preferred_element_type=F32))

    m_c = [jnp.maximum(cols[:, h * LANES:(h + 1) * LANES], m_old[h]) for h in hs]

    num, den = [], []
    for h in hs:
        w = jnp.exp(jnp.where(causal, b_row[h] - jnp.tile(m_c[h], (1, rep_l)), -jnp.inf))
        p = (s[h] * w).astype(BF16)
        w_inter = jnp.exp(m_old[h] - m_c[h])
        qw = (qh[h].astype(F32) * jnp.tile(w_inter, (1, rep_k))).astype(BF16)
        lhs = jnp.concatenate([p, qw], axis=1)
        num.append(jnp.dot(lhs, jnp.concatenate([vh[h], c_old[h].astype(BF16)], axis=0), preferred_element_type=F32))
        den.append(jnp.dot(lhs, jnp.concatenate([ones, n_old[h].astype(BF16)], axis=0), preferred_element_type=F32))

    ssq = [jnp.dot((num[h] * num[h]).astype(BF16), ones_v, preferred_element_type=F32) for h in hs]
    for h in hs:
        g_c = cols[:, (H + h) * LANES:(H + h + 1) * LANES]
        inv = 1.0 / jnp.maximum(jnp.abs(den[h]), jnp.exp(-(g_c + m_c[h])))
        scale = inv * lax.rsqrt(inv * inv * ssq[h] * (1.0 / dv) + NORM_EPS)
        out = num[h] * jnp.tile(scale, (1, rep_v)) * og_ref[0, :, h * dv:(h + 1) * dv].astype(F32)
        out_ref[0, :, h * dv:(h + 1) * dv] = out.astype(out_ref.dtype)

    for h in hs:
        c_sc[h] = c_new[h]
        n_sc[h] = n_new[h]
        m_sc[h] = g[h:h + 1, L - 1:L] + m_end[h]
    cols_sc[...] = gate_cols(b_next, g_next)
    rows_sc[...] = jnp.concatenate([b_next, g_next], axis=0)


def _mlstm(voq, kt, gates, *, heads, dk, dv, chunk):
    bsz, s, _ = voq.shape
    qc, vc = heads * dk, heads * dv
    kern = functools.partial(_mlstm_kernel, heads=heads, dk=dk, dv=dv)
    return pl.pallas_call(
        kern,
        grid=(bsz, s // chunk),
        in_specs=[
            pl.BlockSpec((1, chunk, vc), lambda b, i: (b, i, 0)),
            pl.BlockSpec((1, chunk, vc), lambda b, i: (b, i, 1)),
            pl.BlockSpec((1, chunk, qc), lambda b, i: (b, i, (2 * vc) // qc)),
            pl.BlockSpec((1, qc, chunk), lambda b, i: (b, 0, i)),
            pl.BlockSpec((1, s // chunk, 2 * heads, chunk), lambda b, i: (b, 0, 0, 0)),
            pl.BlockSpec((1, 1, 2 * heads, chunk), lambda b, i: (jnp.minimum(b + 1, bsz - 1), 0, 0, 0)),
        ],
        out_specs=pl.BlockSpec((1, chunk, vc), lambda b, i: (b, i, 0)),
        out_shape=jax.ShapeDtypeStruct((bsz, s, vc), BF16),
        scratch_shapes=[
            pltpu.VMEM((heads, dk, dv), F32),
            pltpu.VMEM((heads, dk, LANES), F32),
            pltpu.VMEM((heads, 1, 1), F32),
            pltpu.VMEM((2 * heads, chunk), F32),
            pltpu.VMEM((chunk, 2 * heads * LANES), F32),
        ],
        compiler_params=_params(("arbitrary", "arbitrary")),
        name="mlstm",
    )(voq, voq, voq, kt, gates, gates)


def _attn_kernel(q_ref, k_ref, vt_ref, lam_ref, g_ref, cap_ref, o_ref, m_sc, acc_sc, s_sc, *, tq, tk, hd, nh, ahead, lam_init):
    lp = lam_ref[...]
    lam = (jnp.exp(jnp.sum(lp[0:1] * lp[1:2], axis=-1, keepdims=True))
           - jnp.exp(jnp.sum(lp[2:3] * lp[3:4], axis=-1, keepdims=True)) + lam_init)
    g_col = g_ref[...] * (1.0 - lam_init)
    comp_dim = hd // 2
    assert tq == tk and ahead <= nh

    def q_tile(qi, carry):
        q0 = pl.multiple_of(qi * tq, tq)
        def stacked_queries(h):
            q = q_ref[0, pl.ds(q0, tq), h * hd:(h + 1) * hd]
            lane = lax.broadcasted_iota(jnp.int32, q.shape, 1)
            zero = jnp.zeros_like(q)
            return jnp.concatenate([jnp.where(lane < comp_dim, q, zero), jnp.where(lane >= comp_dim, q, zero)], axis=0)

        qqs = [stacked_queries(h) for h in range(ahead)]
        ones = jnp.ones((acc_sc.shape[1] - hd, tk), BF16)

        def scores(k0, h):
            kk = k_ref[0, pl.ds(k0, tk), h * hd:(h + 1) * hd]
            return lax.dot_general(kk, qqs[h], _NT, preferred_element_type=F32)

        def step(j, c, diagonal):
            k0 = pl.multiple_of(j * tk, tk)
            pending = [s_sc[i] for i in range(ahead)]
            for h in range(nh):
                s = pending.pop(0)
                if diagonal:
                    s = jnp.minimum(s, cap_ref[...])
                if h + ahead < nh:
                    pending.append(scores(k0, h + ahead))
                elif not diagonal:
                    s_sc[h + ahead - nh] = scores(k0 + tk, h + ahead - nh)
                vt = jnp.concatenate([vt_ref[0, h * hd:(h + 1) * hd, pl.ds(k0, tk)], ones], axis=0)
                m_old = m_sc[h]
                m_new = jnp.maximum(m_old, jnp.max(s, axis=0, keepdims=True))
                alpha = jnp.exp2(m_old - m_new)
                p = jnp.exp2(s - m_new).astype(BF16)
                acc_sc[h] = alpha * acc_sc[h] + jnp.dot(vt, p, preferred_element_type=F32)
                m_sc[h] = m_new
            return c

        def two_steps(jj, c):
            step(2 * jj, c, False)
            return step(2 * jj + 1, c, False)

        for i in range(ahead):
            s_sc[i] = scores(0, i)
        qqs.extend(stacked_queries(h) for h in range(ahead, nh))
        m_sc[...] = jnp.full_like(m_sc, -jnp.inf)
        acc_sc[...] = jnp.zeros_like(acc_sc)
        lax.fori_loop(0, qi // 2, two_steps, 0)
        lax.fori_loop(2 * (qi // 2), qi, functools.partial(step, diagonal=False), 0)
        lax.fori_loop(qi, qi + 1, functools.partial(step, diagonal=True), 0)

        for h in range(nh):
            acc = acc_sc[h]
            acc = acc[:hd] / acc[hd:hd + 1]
            o = acc[:, :tq] - lam * acc[:, tq:]
            o = o * lax.rsqrt(jnp.mean(o * o, axis=0, keepdims=True) + SUBLN_EPS) * g_col
            o_ref[0, pl.ds(q0, tq), h * hd:(h + 1) * hd] = o.T.astype(o_ref.dtype)
        return carry

    lax.fori_loop(0, q_ref.shape[1] // tq, q_tile, 0)


def _attention(q, k, vt, lam_params, subln_g, *, heads, lam_init):
    bsz, s, d = q.shape
    hd = d // heads
    tq = _tile(s, 256)
    tk = tq
    nh = _tile(heads, 8)
    kern = functools.partial(_attn_kernel, tq=tq, tk=tk, hd=hd, nh=nh, ahead=ATTN_AHEAD, lam_init=lam_init)
    kpos = lax.broadcasted_iota(jnp.int32, (tk, 2 * tq), 0)
    qpos = lax.broadcasted_iota(jnp.int32, (tk, 2 * tq), 1) % tq
    cap = jnp.where(kpos <= qpos, jnp.inf, -jnp.inf).astype(F32)
    return pl.pallas_call(
        kern,
        grid=(bsz, heads // nh),
        in_specs=[
            pl.BlockSpec((1, s, nh * hd), lambda b, h: (b, 0, h)),
            pl.BlockSpec((1, s, nh * hd), lambda b, h: (b, 0, h)),
            pl.BlockSpec((1, nh * hd, s), lambda b, h: (b, h, 0)),
            _resident(lam_params.shape),
            _resident((hd, 1)),
            _resident(cap.shape),
        ],
        out_specs=pl.BlockSpec((1, s, nh * hd), lambda b, h: (b, 0, h)),
        out_shape=jax.ShapeDtypeStruct((bsz, s, d), BF16),
        scratch_shapes=[
            pltpu.VMEM((nh, 1, 2 * tq), F32),
            pltpu.VMEM((nh, hd + BF16_SUBLANES, 2 * tq), F32),
            pltpu.VMEM((ATTN_AHEAD, tk, 2 * tq), F32),
        ],
        compiler_params=_params(("arbitrary", "arbitrary")),
        name="diff_attn",
    )(q, k, vt, lam_params, subln_g.reshape(hd, 1), cap)


def _out_mlp_kernel(x_ref, a_ref, g1_ref, sh_ref, sc_ref, g2_ref, ng_ref, wo_ref, wu_ref, wd_ref, fg_ref, o_ref, *, fc, final):
    x1 = x_ref[0] + g1_ref[0] * jnp.dot(a_ref[0], wo_ref[...], preferred_element_type=F32)
    y = _norm_mod(x1, ng_ref[...], sh_ref[0], sc_ref[0]).astype(BF16)
    acc = jnp.zeros_like(x1)
    for c in range(wu_ref.shape[1] // fc):
        h = jnp.dot(y, wu_ref[:, c * fc:(c + 1) * fc], preferred_element_type=F32)
        h = jnp.square(jnp.maximum(h, 0.0)).astype(BF16)
        acc = acc + jnp.dot(h, wd_ref[c * fc:(c + 1) * fc, :], preferred_element_type=F32)
    x2 = x1 + g2_ref[0] * acc
    if final:
        x2 = _rms(x2, NORM_EPS) * fg_ref[...]
    o_ref[0] = x2


def _out_mlp(x, a, gate1, shift2, scale2, gate2, norm_g, w_out, w_up, w_down, final_g, *, final):
    bsz, s, d = x.shape
    da = a.shape[-1]
    f = w_up.shape[1]
    tm = _tile(s, 1024)
    kern = functools.partial(_out_mlp_kernel, fc=_tile(f, 512), final=final)
    vec = pl.BlockSpec((1, 1, d), lambda b, i: (b, 0, 0))
    return pl.pallas_call(
        kern,
        grid=(bsz, s // tm),
        in_specs=[
            pl.BlockSpec((1, tm, d), lambda b, i: (b, i, 0)),
            pl.BlockSpec((1, tm, da), lambda b, i: (b, i, 0)),
            vec, vec, vec, vec,
            _resident((1, d)),
            _resident((da, d)),
            _resident((d, f)),
            _resident((f, d)),
            _resident((1, d)),
        ],
        out_specs=pl.BlockSpec((1, tm, d), lambda b, i: (b, i, 0)),
        out_shape=jax.ShapeDtypeStruct((bsz, s, d), F32),
        compiler_params=_params(("arbitrary", "arbitrary")),
        name="out_mlp",
    )(x, a, gate1, shift2, scale2, gate2, norm_g.reshape(1, d), w_out, w_up, w_down, final_g.reshape(1, d))


def _lambda_init(layer_number):
    return 0.8 - 0.6 * math.exp(-0.3 * (layer_number - 1))


def kernel(x, c, positions, ada_w, ada_b, norm_mix_g, norm_mlp_g, mlp_w_up, mlp_w_down, mlstm_w_in, mlstm_b_gates, mlstm_head_g, mlstm_w_out, kv_norm_g, kv_ada_w, kv_ada_b, kv_w, diff_w_q, diff_lambda, diff_subln_g, diff_w_out, final_norm_g):
    bsz, s, d = x.shape
    depth = ada_w.shape[0]
    n_a = mlstm_w_in.shape[0]
    a_heads = mlstm_b_gates.shape[-1]
    a_v_cols = mlstm_w_out.shape[1]
    a_q_cols = (mlstm_w_in.shape[-1] - 2 * a_v_cols - 2 * a_heads) // 2
    a_dk, a_dv = a_q_cols // a_heads, a_v_cols // a_heads
    b_comp = diff_lambda.shape[-1]
    b_heads = diff_w_q.shape[-1] // (2 * b_comp)
    rope_dim = b_comp // 4

    mods = _mods(c, ada_w, ada_b).reshape(depth, bsz, 1, 6 * d)
    kv_mods = _mods(c, kv_ada_w[None], kv_ada_b[None]).reshape(bsz, 1, 2 * d)

    k = vt = None
    for layer in range(depth):
        m = mods[layer]
        sh1, sc1, g1, sh2, sc2, g2 = [m[:, :, j * d:(j + 1) * d] for j in range(6)]
        if layer < n_a:
            w_in = mlstm_w_in[layer]
            n_main = 2 * a_q_cols + 2 * a_v_cols
            w_voq = jnp.concatenate([w_in[:, 2 * a_q_cols:n_main], w_in[:, :a_q_cols]], axis=1).astype(BF16)
            w_kt = w_in[:, a_q_cols:2 * a_q_cols].T.astype(BF16)
            w_gates_t = w_in[:, n_main:].T.astype(BF16)
            b_g = mlstm_b_gates[layer].reshape(2 * a_heads, 1)
            chunk = _tile(s, MLSTM_CHUNK)
            voq, kt, gates = _mlstm_proj(x, norm_mix_g[layer], sh1, sc1, w_voq, w_kt, w_gates_t, b_g, mlstm_head_g[layer],
                                         v_cols=a_v_cols, k_scale=a_dk ** -0.5, heads=a_heads, chunk=chunk)
            mix = _mlstm(voq, kt, gates, heads=a_heads, dk=a_dk, dv=a_dv, chunk=chunk)
            w_o = mlstm_w_out[layer]
        else:
            j = layer - n_a
            (q,) = _rope_proj(x, norm_mix_g[layer], sh1, sc1, diff_w_q[j].astype(BF16), positions,
                              out_scale=b_comp ** -0.5 * LOG2E, half=rope_dim // 2)
            mix = _attention(q, k, vt, diff_lambda[j], diff_subln_g[j], heads=b_heads, lam_init=_lambda_init(layer + 1))
            w_o = diff_w_out[j]
        x = _out_mlp(x, mix, g1, sh2, sc2, g2, norm_mlp_g[layer], w_o.astype(BF16), mlp_w_up[layer].astype(BF16),
                     mlp_w_down[layer].astype(BF16), final_norm_g, final=(layer == depth - 1))
        if layer == n_a - 1:
            n_k = b_heads * 2 * b_comp
            k, vt = _rope_proj(x, kv_norm_g, kv_mods[:, :, :d], kv_mods[:, :, d:], kv_w[:, :n_k].astype(BF16), positions,
                               kv_w[:, n_k:].T.astype(BF16), out_scale=1.0, half=rope_dim // 2)
    return x
```
